```python
import math
import jax, jax.numpy as jnp
from jax import lax
import numpy as np

D_MODEL = 1024
BATCH = 4
SEQ = 8192
DEPTH = 2

MLA_HEADS = 8
MLA_NOPE = 64
MLA_ROPE = 32
MLA_QK = MLA_NOPE + MLA_ROPE
MLA_V = 64
MLA_Q_RANK = 384
MLA_KV_RANK = 256
ROPE_BASE = 10000.0
Q_BLOCK = 128

GLA_HEADS = 4
GLA_DK = 64
GLA_DV = 128
GLA_GATE_RANK = 16
GLA_TAU = 16.0
GLA_CHUNK = 64

S5_CH = 512
S5_GROUP = 16
S5_GROUPS = S5_CH // S5_GROUP
S5_STATE = 64
S5_DT_MIN = 0.001
S5_DT_MAX = 0.1

N_BRANCH = 3
BRANCH_W = 512
D_FF = 4 * D_MODEL
EPS = 1e-6

IN_SIZES = (MLA_Q_RANK, MLA_KV_RANK, MLA_ROPE,
            GLA_HEADS * GLA_DK, GLA_HEADS * GLA_DK, GLA_HEADS * GLA_DV, GLA_GATE_RANK, GLA_HEADS * GLA_DV,
            S5_CH,
            N_BRANCH * D_MODEL)
D_IN = sum(IN_SIZES)

kernel_name = 'hybrid_mla_gla_s5_gated_block'


def rms_norm(x, g):
    xf = x.astype(jnp.float32)
    y = xf * lax.rsqrt(jnp.mean(xf * xf, axis=-1, keepdims=True) + EPS)
    return (y * g.astype(jnp.float32)).astype(x.dtype)


def rope(x, cos, sin):
    x1, x2 = jnp.split(x, 2, axis=-1)
    return jnp.concatenate([x1 * cos - x2 * sin, x1 * sin + x2 * cos], axis=-1)


def split_in(z):
    idx = []
    acc = 0
    for s in IN_SIZES[:-1]:
        acc += s
        idx.append(acc)
    return jnp.split(z, idx, axis=-1)


def mla_mixer(h_cq, h_ckv, k_pe, q_norm_g, w_uq, kv_norm_g, w_ukv, q_head_g, k_head_g, cos, sin):
    B, L, _ = h_cq.shape
    dt = h_cq.dtype
    q = (rms_norm(h_cq, q_norm_g) @ w_uq).reshape(B, L, MLA_HEADS, MLA_QK)
    kv = (rms_norm(h_ckv, kv_norm_g) @ w_ukv).reshape(B, L, MLA_HEADS, MLA_NOPE + MLA_V)
    k_nope, v = kv[..., :MLA_NOPE], kv[..., MLA_NOPE:]
    k = jnp.concatenate([k_nope, jnp.broadcast_to(k_pe[:, :, None, :], (B, L, MLA_HEADS, MLA_ROPE))], axis=-1)
    q = rms_norm(q, q_head_g)
    k = rms_norm(k, k_head_g)
    c4, s4 = cos[:, None, :].astype(dt), sin[:, None, :].astype(dt)
    q = jnp.concatenate([q[..., :MLA_NOPE], rope(q[..., MLA_NOPE:], c4, s4)], axis=-1)
    k = jnp.concatenate([k[..., :MLA_NOPE], rope(k[..., MLA_NOPE:], c4, s4)], axis=-1)
    q = q.transpose(0, 2, 1, 3)
    k = k.transpose(0, 2, 1, 3)
    v = v.transpose(0, 2, 1, 3)
    n_blk = L // Q_BLOCK
    qb = q.reshape(B, MLA_HEADS, n_blk, Q_BLOCK, MLA_QK).transpose(2, 0, 1, 3, 4)
    key_pos = jnp.arange(L)
    scale = MLA_QK ** -0.5

    def attend(args):
        qi, blk = args
        s = jnp.einsum('bhqd,bhkd->bhqk', qi, k).astype(jnp.float32) * scale
        q_pos = blk * Q_BLOCK + jnp.arange(Q_BLOCK)
        mask = key_pos[None, :] <= q_pos[:, None]
        s = jnp.where(mask, s, jnp.finfo(jnp.float32).min)
        p = jax.nn.softmax(s, axis=-1)
        return jnp.einsum('bhqk,bhkd->bhqd', p.astype(v.dtype), v)

    o = lax.map(attend, (qb, jnp.arange(n_blk)))
    return o.transpose(1, 0, 3, 2, 4).reshape(B, L, MLA_HEADS * MLA_V)


def gla_mixer(q, k, v, g_lr, r, w_gate, b_gate, out_g):
    B, L, _ = q.shape
    dt = q.dtype
    n_ch = L // GLA_CHUNK
    f32 = jnp.float32

    def chunks(t, d):
        return t.astype(f32).reshape(B, n_ch, GLA_CHUNK, GLA_HEADS, d).transpose(0, 3, 1, 2, 4)

    log_a = jax.nn.log_sigmoid(g_lr.astype(f32) @ w_gate.astype(f32) + b_gate.astype(f32)) / GLA_TAU
    qc = chunks(q, GLA_DK) * (GLA_DK ** -0.5)
    kc = chunks(k, GLA_DK)
    vc = chunks(v, GLA_DV)
    bc = jnp.cumsum(chunks(log_a, GLA_DK), axis=3)
    b_last = bc[..., -1:, :]
    q_t = qc * jnp.exp(bc)
    k_t = kc * jnp.exp(-bc)
    k_end = kc * jnp.exp(b_last - bc)
    causal = jnp.tril(jnp.ones((GLA_CHUNK, GLA_CHUNK), dtype=bool))
    a_intra = jnp.where(causal, jnp.einsum('bhncd,bhnsd->bhncs', q_t, k_t), 0.0)
    o_intra = jnp.einsum('bhncs,bhnse->bhnce', a_intra, vc)
    d_state = jnp.einsum('bhncd,bhnce->bhnde', k_end, vc)
    decay = jnp.exp(b_last[..., 0, :])

    def step(S, inp):
        dS_n, dec_n = inp
        return dec_n[..., None] * S + dS_n, S

    S0 = jnp.zeros((B, GLA_HEADS, GLA_DK, GLA_DV), f32)
    _, S_prev = lax.scan(step, S0, (jnp.moveaxis(d_state, 2, 0), jnp.moveaxis(decay, 2, 0)))
    S_prev = jnp.moveaxis(S_prev, 0, 2)
    o = o_intra + jnp.einsum('bhncd,bhnde->bhnce', q_t, S_prev)
    o = o.transpose(0, 2, 3, 1, 4).reshape(B, L, GLA_HEADS, GLA_DV)
    o = rms_norm(o, out_g) * jax.nn.silu(r.astype(f32)).reshape(B, L, GLA_HEADS, GLA_DV)
    return o.reshape(B, L, GLA_HEADS * GLA_DV).astype(dt)


def diag_combine(e1, e2):
    a1, b1 = e1
    a2, b2 = e2
    return a1 * a2, a2 * b1 + b2


def s5_mixer(u, lam_re, lam_im, b_re, b_im, c_re, c_im, d, log_dt, w_glu, b_glu):
    B, L, _ = u.shape
    f32 = jnp.float32
    uf = u.astype(f32).reshape(B, L, S5_GROUPS, S5_GROUP)
    lam = lax.complex(jnp.minimum(lam_re.astype(f32), -1e-4), lam_im.astype(f32))
    step = jnp.exp(log_dt.astype(f32))[:, None]
    lam_bar = jnp.exp(lam * step)
    b_bar = ((lam_bar - 1.0) / lam)[..., None] * lax.complex(b_re.astype(f32), b_im.astype(f32))
    bu = lax.complex(jnp.einsum('blgi,gpi->blgp', uf, jnp.real(b_bar)),
                     jnp.einsum('blgi,gpi->blgp', uf, jnp.imag(b_bar)))
    a = jnp.broadcast_to(lam_bar, bu.shape)
    _, states = lax.associative_scan(diag_combine, (a, bu), axis=1)
    y = (jnp.einsum('blgp,gip->blgi', jnp.real(states), c_re.astype(f32))
         - jnp.einsum('blgp,gip->blgi', jnp.imag(states), c_im.astype(f32))
         + d.astype(f32) * uf)
    y = jax.nn.gelu(y.reshape(B, L, S5_CH))
    y = y * jax.nn.sigmoid(y @ w_glu.astype(f32) + b_glu.astype(f32))
    return y.astype(u.dtype)


def setup_inputs(seed: int = 0) -> dict:
    key = jax.random.key(seed)
    ks = iter(jax.random.split(key, 40))
    f32 = jnp.float32

    def nrm(shape, scale):
        return jax.random.normal(next(ks), shape, f32) * scale

    def gain(shape):
        return 1.0 + 0.02 * jax.random.normal(next(ks), shape, f32)

    n_idx = jnp.arange(S5_STATE, dtype=f32)
    return {
        'x': nrm((BATCH, SEQ, D_MODEL), 1.0),
        'norm1_g': gain((DEPTH, D_MODEL)),
        'w_in': nrm((DEPTH, D_MODEL, D_IN), D_MODEL ** -0.5),
        'mla_q_norm_g': gain((DEPTH, MLA_Q_RANK)),
        'mla_w_uq': nrm((DEPTH, MLA_Q_RANK, MLA_HEADS * MLA_QK), MLA_Q_RANK ** -0.5),
        'mla_kv_norm_g': gain((DEPTH, MLA_KV_RANK)),
        'mla_w_ukv': nrm((DEPTH, MLA_KV_RANK, MLA_HEADS * (MLA_NOPE + MLA_V)), MLA_KV_RANK ** -0.5),
        'mla_q_head_g': gain((DEPTH, MLA_QK)),
        'mla_k_head_g': gain((DEPTH, MLA_QK)),
        'gla_w_gate': nrm((DEPTH, GLA_GATE_RANK, GLA_HEADS * GLA_DK), GLA_GATE_RANK ** -0.5),
        'gla_b_gate': nrm((DEPTH, GLA_HEADS * GLA_DK), 0.1),
        'gla_out_g': gain((DEPTH, GLA_DV)),
        's5_lam_re': -0.5 + nrm((DEPTH, S5_GROUPS, S5_STATE), 0.01),
        's5_lam_im': jnp.pi * n_idx + nrm((DEPTH, S5_GROUPS, S5_STATE), 0.01),
        's5_b_re': nrm((DEPTH, S5_GROUPS, S5_STATE, S5_GROUP), (2 * S5_GROUP) ** -0.5),
        's5_b_im': nrm((DEPTH, S5_GROUPS, S5_STATE, S5_GROUP), (2 * S5_GROUP) ** -0.5),
        's5_c_re': nrm((DEPTH, S5_GROUPS, S5_GROUP, S5_STATE), S5_STATE ** -0.5),
        's5_c_im': nrm((DEPTH, S5_GROUPS, S5_GROUP, S5_STATE), S5_STATE ** -0.5),
        's5_d': nrm((DEPTH, S5_GROUPS, S5_GROUP), 1.0),
        's5_log_dt': jax.random.uniform(next(ks), (DEPTH, S5_GROUPS), f32,
                                        minval=math.log(S5_DT_MIN), maxval=math.log(S5_DT_MAX)),
        's5_w_glu': nrm((DEPTH, S5_CH, S5_CH), S5_CH ** -0.5),
        's5_b_glu': nrm((DEPTH, S5_CH), 0.01),
        'w_br_mla': nrm((DEPTH, BRANCH_W, D_MODEL), BRANCH_W ** -0.5),
        'w_br_gla': nrm((DEPTH, BRANCH_W, D_MODEL), BRANCH_W ** -0.5),
        'w_br_s5': nrm((DEPTH, BRANCH_W, D_MODEL), BRANCH_W ** -0.5),
        'gate_b': nrm((DEPTH, N_BRANCH * D_MODEL), 0.01),
        'w_out': nrm((DEPTH, D_MODEL, D_MODEL), D_MODEL ** -0.5),
        'norm2_g': gain((DEPTH, D_MODEL)),
        'w_ff1': nrm((DEPTH, D_MODEL, D_FF), D_MODEL ** -0.5),
        'w_ff2': nrm((DEPTH, D_FF, D_MODEL), D_FF ** -0.5),
    }


def reference(x, norm1_g, w_in, mla_q_norm_g, mla_w_uq, mla_kv_norm_g, mla_w_ukv, mla_q_head_g,
              mla_k_head_g, gla_w_gate, gla_b_gate, gla_out_g, s5_lam_re, s5_lam_im, s5_b_re, s5_b_im,
              s5_c_re, s5_c_im, s5_d, s5_log_dt, s5_w_glu, s5_b_glu, w_br_mla, w_br_gla, w_br_s5, gate_b,
              w_out, norm2_g, w_ff1, w_ff2):
    B, L, D = x.shape
    pos = jnp.arange(L, dtype=jnp.float32)
    inv_freq = ROPE_BASE ** (-jnp.arange(0, MLA_ROPE, 2, dtype=jnp.float32) / MLA_ROPE)
    ang = pos[:, None] * inv_freq[None, :]
    cos, sin = jnp.cos(ang), jnp.sin(ang)
    for l in range(DEPTH):
        h = rms_norm(x, norm1_g[l])
        z = h @ w_in[l]
        (cq, ckv, kpe, gq, gk, gv, glr, gr, su, gates) = split_in(z)
        kpe = rope(kpe, cos.astype(kpe.dtype), sin.astype(kpe.dtype)) * 1.0 if False else kpe
        o_a = mla_mixer(cq, ckv, kpe, mla_q_norm_g[l], mla_w_uq[l], mla_kv_norm_g[l], mla_w_ukv[l],
                        mla_q_head_g[l], mla_k_head_g[l], cos, sin)
        o_b = gla_mixer(gq, gk, gv, glr, gr, gla_w_gate[l], gla_b_gate[l], gla_out_g[l])
        o_c = s5_mixer(su, s5_lam_re[l], s5_lam_im[l], s5_b_re[l], s5_b_im[l], s5_c_re[l], s5_c_im[l],
                       s5_d[l], s5_log_dt[l], s5_w_glu[l], s5_b_glu[l])
        g = jax.nn.sigmoid(gates + gate_b[l]).reshape(B, L, N_BRANCH, D)
        merged = (g[:, :, 0] * (o_a @ w_br_mla[l])
                  + g[:, :, 1] * (o_b @ w_br_gla[l])
                  + g[:, :, 2] * (o_c @ w_br_s5[l]))
        x = x + merged @ w_out[l]
        h2 = rms_norm(x, norm2_g[l])
        x = x + jnp.square(jax.nn.relu(h2 @ w_ff1[l])) @ w_ff2[l]
    return x
```

```python
import functools
import math

import numpy as np
import jax
import jax.numpy as jnp
from jax import lax
from jax.experimental import pallas as pl
from jax.experimental.pallas import tpu as pltpu

F32 = jnp.float32
BF16 = jnp.bfloat16
HIGHEST = lax.Precision.HIGHEST

D_MODEL = 1024
MLA_HEADS = 8
MLA_NOPE = 64
MLA_ROPE = 32
MLA_QK = MLA_NOPE + MLA_ROPE
MLA_V = 64
MLA_Q_RANK = 384
MLA_KV_RANK = 256
ROPE_BASE = 10000.0
GLA_HEADS = 4
GLA_DK = 64
GLA_DV = 128
GLA_GATE_RANK = 16
GLA_TAU = 16.0
GLA_CHUNK = 64
S5_CH = 512
S5_GROUP = 16
S5_GROUPS = S5_CH // S5_GROUP
S5_STATE = 64
N_BRANCH = 3
BRANCH_W = 512
D_FF = 4 * D_MODEL
EPS = 1e-6

LANES = 128
HEAD_SLOT = LANES
S5_T = 16
S5_XW = S5_T * S5_GROUP
VMEM_LIMIT = 56 * 1024 * 1024
NEG = -1e30

ZMLA_W = MLA_Q_RANK + MLA_KV_RANK + 2 * LANES
ZGLA_W = 2 * GLA_HEADS * GLA_DK + 2 * GLA_HEADS * GLA_DV + LANES
GATE_W = N_BRANCH * D_MODEL


def _cparams(sem):
    return pltpu.CompilerParams(dimension_semantics=sem, vmem_limit_bytes=VMEM_LIMIT)


def _const_spec(shape):
    nd = len(shape)
    return pl.BlockSpec(shape, lambda *_: (0,) * nd)


def _dot(a, b):
    return jnp.dot(a, b, preferred_element_type=F32)


def _dot_nt(a, b):
    return lax.dot_general(a, b, (((1,), (1,)), ((), ())), preferred_element_type=F32)


def _in_proj_kernel(x_ref, g_ref, w_ref, gb_ref, zmla_ref, zgla_ref, su_ref, gate_ref, h_ref, *, chunk):
    x = x_ref[...]
    ms = jnp.mean(x * x, axis=-1, keepdims=True)
    h_ref[...] = (x * lax.rsqrt(ms + EPS) * g_ref[...]).astype(BF16)
    col = 0
    for out_ref in (zmla_ref, zgla_ref, su_ref):
        width = out_ref.shape[-1]
        for c0 in range(0, width, chunk):
            c1 = min(c0 + chunk, width)
            out_ref[:, c0:c1] = _dot(h_ref[...], w_ref[:, col + c0:col + c1]).astype(BF16)
        col += width
    for c0 in range(0, GATE_W, chunk):
        pre = _dot(h_ref[...], w_ref[:, col + c0:col + c0 + chunk]) + gb_ref[:, c0:c0 + chunk]
        gate_ref[:, c0:c0 + chunk] = jax.nn.sigmoid(pre).astype(BF16)


def _in_proj(xf, g, w, gate_b, tn):
    n = xf.shape[0]
    wtot = w.shape[1]
    return pl.pallas_call(
        functools.partial(_in_proj_kernel, chunk=512),
        grid=(n // tn,),
        in_specs=[
            pl.BlockSpec((tn, D_MODEL), lambda i: (i, 0)),
            _const_spec((1, D_MODEL)),
            _const_spec((D_MODEL, wtot)),
            _const_spec((1, GATE_W)),
        ],
        out_specs=[
            pl.BlockSpec((tn, ZMLA_W), lambda i: (i, 0)),
            pl.BlockSpec((tn, ZGLA_W), lambda i: (i, 0)),
            pl.BlockSpec((tn, S5_CH), lambda i: (i, 0)),
            pl.BlockSpec((tn, GATE_W), lambda i: (i, 0)),
        ],
        out_shape=[
            jax.ShapeDtypeStruct((n, ZMLA_W), BF16),
            jax.ShapeDtypeStruct((n, ZGLA_W), BF16),
            jax.ShapeDtypeStruct((n, S5_CH), BF16),
            jax.ShapeDtypeStruct((n, GATE_W), BF16),
        ],
        scratch_shapes=[pltpu.VMEM((tn, D_MODEL), BF16)],
        compiler_params=_cparams(("arbitrary",)),
        name="in_proj",
    )(xf, g, w, gate_b)


def _mla_prep_kernel(z_ref, cos_ref, sin_ref, gqn_ref, gkvn_ref, wq_ref, wqs_ref, wk_ref, wv_ref,
                     bd_ref, gq_ref, gqs_ref, gk_ref, gks_ref, vone_ref, q_ref, k_ref, v_ref):
    cq = z_ref[0, :, 0:MLA_Q_RANK].astype(F32)
    ckv = z_ref[0, :, MLA_Q_RANK:MLA_Q_RANK + MLA_KV_RANK].astype(F32)
    o = MLA_Q_RANK + MLA_KV_RANK
    kpe = z_ref[0, :, o:o + LANES].astype(F32)
    kpe_sw = z_ref[0, :, o + LANES:o + 2 * LANES].astype(F32)
    cos = cos_ref[...]
    sin = sin_ref[...]

    cqn = (cq * lax.rsqrt(jnp.mean(cq * cq, axis=-1, keepdims=True) + EPS) * gqn_ref[...]).astype(BF16)
    ckvn = (ckv * lax.rsqrt(jnp.mean(ckv * ckv, axis=-1, keepdims=True) + EPS) * gkvn_ref[...]).astype(BF16)

    q_raw = _dot(cqn, wq_ref[...])
    q_sw = _dot(cqn, wqs_ref[...])
    k_nope = _dot(ckvn, wk_ref[...])
    v_all = _dot(ckvn, wv_ref[...]) + vone_ref[...]

    bd = bd_ref[...]
    ssq_q = _dot((q_raw * q_raw).astype(BF16), bd)
    ssq_kn = _dot((k_nope * k_nope).astype(BF16), bd)
    ssq_pe = _dot((kpe * kpe).astype(BF16), bd[0:LANES, 0:LANES])
    rq = lax.rsqrt(ssq_q * (1.0 / MLA_QK) + EPS) * (MLA_QK ** -0.5)

    cq_t = gq_ref[...] * cos
    sq_t = gqs_ref[...] * sin
    ck_t = gk_ref[...] * cos
    kpe_rot = kpe * ck_t + kpe_sw * (gks_ref[...] * sin)
    for h in range(MLA_HEADS):
        sl = slice(h * HEAD_SLOT, (h + 1) * HEAD_SLOT)
        qh = rq[:, sl] * (q_raw[:, sl] * cq_t + q_sw[:, sl] * sq_t)
        q_ref[0, h] = qh.astype(BF16)
        rk = lax.rsqrt((ssq_kn[:, sl] + ssq_pe) * (1.0 / MLA_QK) + EPS)
        kh = rk * (k_nope[:, sl] * ck_t + kpe_rot)
        k_ref[0, h] = kh.astype(BF16)
        v_ref[0, h] = v_all[:, sl].astype(BF16)


def _mla_prep(zmla, cos128, sin128, p, tl):
    b, l, _ = zmla.shape
    hw = MLA_HEADS * HEAD_SLOT
    head_out = jax.ShapeDtypeStruct((b, MLA_HEADS, l, HEAD_SLOT), BF16)
    head_spec = pl.BlockSpec((1, MLA_HEADS, tl, HEAD_SLOT), lambda i, j: (i, 0, j, 0))
    return pl.pallas_call(
        _mla_prep_kernel,
        grid=(b, l // tl),
        in_specs=[
            pl.BlockSpec((1, tl, ZMLA_W), lambda i, j: (i, j, 0)),
            pl.BlockSpec((tl, LANES), lambda i, j: (j, 0)),
            pl.BlockSpec((tl, LANES), lambda i, j: (j, 0)),
            _const_spec((1, MLA_Q_RANK)),
            _const_spec((1, MLA_KV_RANK)),
            _const_spec((MLA_Q_RANK, hw)),
            _const_spec((MLA_Q_RANK, hw)),
            _const_spec((MLA_KV_RANK, hw)),
            _const_spec((MLA_KV_RANK, hw)),
            _const_spec((hw, hw)),
            _const_spec((1, LANES)),
            _const_spec((1, LANES)),
            _const_spec((1, LANES)),
            _const_spec((1, LANES)),
            _const_spec((1, hw)),
        ],
        out_specs=[head_spec, head_spec, head_spec],
        out_shape=[head_out, head_out, head_out],
        compiler_params=_cparams(("arbitrary", "arbitrary")),
        name="mla_prep",
    )(zmla, cos128, sin128, p["gqn"], p["gkvn"], p["wq"], p["wqs"], p["wk"], p["wv"], p["bd_head"],
      p["gq"], p["gqs"], p["gk"], p["gks"], p["vone"])


def _flash_kernel(q_ref, k_ref, v_ref, o_ref, *, tq):
    qi = pl.program_id(2)
    row = lax.broadcasted_iota(jnp.int32, (tq, tq), 0)
    col = lax.broadcasted_iota(jnp.int32, (tq, tq), 1)
    lane = lax.broadcasted_iota(jnp.int32, (tq, LANES), 1)
    outs = []
    for hh in range(2):
        q = q_ref[0, hh]

        def step(j, carry, masked, hh=hh, q=q):
            m, acc = carry
            start = pl.multiple_of(j * tq, tq)
            k = k_ref[0, hh, pl.ds(start, tq), :]
            v = v_ref[0, hh, pl.ds(start, tq), :]
            s = _dot_nt(q, k)
            if masked:
                s = jnp.where(col <= row, s, NEG)
            m_new = jnp.maximum(m, jnp.max(s, axis=-1, keepdims=True))
            alpha = jnp.exp(m - m_new)
            p = jnp.exp(s - m_new)
            acc = alpha * acc + _dot(p.astype(BF16), v)
            return m_new, acc

        carry = (jnp.full((tq, 1), NEG, F32), jnp.zeros((tq, LANES), F32))
        carry = lax.fori_loop(0, qi, functools.partial(step, masked=False), carry)
        _, acc = step(qi, carry, True)
        lcol = MLA_V if hh == 0 else 0
        outs.append(acc / acc[:, lcol:lcol + 1])
    o_ref[0] = jnp.where(lane < MLA_V, outs[0], outs[1]).astype(BF16)


def _flash(q, k, v, tq):
    b, h, l, _ = q.shape
    kv_spec = pl.BlockSpec((1, 2, l, HEAD_SLOT), lambda i, j, t: (i, j, 0, 0))
    return pl.pallas_call(
        functools.partial(_flash_kernel, tq=tq),
        grid=(b, h // 2, l // tq),
        in_specs=[pl.BlockSpec((1, 2, tq, HEAD_SLOT), lambda i, j, t: (i, j, t, 0)), kv_spec, kv_spec],
        out_specs=pl.BlockSpec((1, tq, LANES), lambda i, j, t: (i, t, j)),
        out_shape=jax.ShapeDtypeStruct((b, l, h * MLA_V), BF16),
        compiler_params=_cparams(("arbitrary", "arbitrary", "arbitrary")),
        name="flash",
    )(q, k, v)


GLA_QW = GLA_HEADS * GLA_DK
GLA_VW = GLA_HEADS * GLA_DV


def _split_bf16(a):
    hi = a.astype(BF16)
    lo = (a - hi.astype(F32)).astype(BF16)
    return hi, lo


def _gla_kernel(z_ref, wg_ref, bg_ref, og_ref, tri_ref, upp_ref, mk_ref, mv_ref, mvt_ref, causal_ref, bdn_ref,
                o_ref, st_ref, oacc_ref, *, tg):
    @pl.when(pl.program_id(1) == 0)
    def _():
        st_ref[...] = jnp.zeros_like(st_ref)

    q = z_ref[0, :, 0:GLA_QW].astype(F32)
    k = z_ref[0, :, GLA_QW:2 * GLA_QW].astype(F32)
    v = z_ref[0, :, 2 * GLA_QW:2 * GLA_QW + GLA_VW]
    r = z_ref[0, :, 2 * GLA_QW + GLA_VW:2 * GLA_QW + 2 * GLA_VW].astype(F32)
    glr = z_ref[0, :, 2 * GLA_QW + 2 * GLA_VW:]

    pre = _dot(glr, wg_ref[...]) + bg_ref[...]
    la = (jnp.minimum(pre, 0.0) - jnp.log(1.0 + jnp.exp(-jnp.abs(pre)))) * (1.0 / GLA_TAU)
    la_hi, la_lo = _split_bf16(la)
    bc = _dot(tri_ref[...], la_hi) + _dot(tri_ref[...], la_lo)
    rem = _dot(upp_ref[...], la_hi) + _dot(upp_ref[...], la_lo)
    qt = (q * (GLA_DK ** -0.5) * jnp.exp(bc)).astype(BF16)
    kt = (k * jnp.exp(-bc)).astype(BF16)
    kend = (k * jnp.exp(rem)).astype(BF16)

    for c in range(tg // GLA_CHUNK):
        sl = slice(c * GLA_CHUNK, (c + 1) * GLA_CHUNK)
        qc, kc, kec, vc = qt[sl], kt[sl], kend[sl], v[sl]
        krows = jnp.concatenate([kc] * GLA_HEADS, axis=0) * mk_ref[...]
        a = _dot_nt(qc, krows)
        a = jnp.where(causal_ref[...] > 0, a, 0.0).astype(BF16)
        vbd = jnp.concatenate([vc] * GLA_HEADS, axis=0) * mv_ref[...]
        st = st_ref[...]
        o = _dot(a, vbd) + _dot_nt(qc, st.astype(BF16))
        oacc_ref[sl, :] = o
        dst = _dot(vc.astype(F32).T.astype(BF16), kec)
        dec = jnp.exp(bc[(c + 1) * GLA_CHUNK - 1:(c + 1) * GLA_CHUNK, :])
        st_ref[...] = st * dec + dst * mvt_ref[...]

    o = oacc_ref[...]
    ss = _dot((o * o).astype(BF16), bdn_ref[...])
    y = o * lax.rsqrt(ss * (1.0 / GLA_DV) + EPS) * og_ref[...]
    o_ref[0] = (y * (r * jax.nn.sigmoid(r))).astype(BF16)


def _gla(zgla, p, c, tg):
    b, l, _ = zgla.shape
    return pl.pallas_call(
        functools.partial(_gla_kernel, tg=tg),
        grid=(b, l // tg),
        in_specs=[
            pl.BlockSpec((1, tg, ZGLA_W), lambda i, j: (i, j, 0)),
            _const_spec((LANES, GLA_QW)),
            _const_spec((1, GLA_QW)),
            _const_spec((1, GLA_VW)),
            _const_spec((tg, tg)),
            _const_spec((tg, tg)),
            _const_spec((GLA_HEADS * GLA_CHUNK, GLA_QW)),
            _const_spec((GLA_HEADS * GLA_CHUNK, GLA_VW)),
            _const_spec((GLA_VW, GLA_QW)),
            _const_spec((GLA_CHUNK, GLA_HEADS * GLA_CHUNK)),
            _const_spec((GLA_VW, GLA_VW)),
        ],
        out_specs=pl.BlockSpec((1, tg, GLA_VW), lambda i, j: (i, j, 0)),
        out_shape=jax.ShapeDtypeStruct((b, l, GLA_VW), BF16),
        scratch_shapes=[pltpu.VMEM((GLA_VW, GLA_QW), F32), pltpu.VMEM((tg, GLA_VW), F32)],
        compiler_params=_cparams(("arbitrary", "arbitrary")),
        name="gla",
    )(zgla, p["w_gate"], p["b_gate"], p["gla_og"], c["gla_tri"], c["gla_upp"], c["gla_mk"], c["gla_mv"],
      c["gla_mvt"], c["gla_causal"], c["gla_bdn"])


def _s5_state_kernel(x_ref, p_ref, dre_ref, dim_ref):
    x0 = x_ref[0]
    x1 = x_ref[1]
    dre_ref[...] = _dot(x0, p_ref[0, 0, 0]) + _dot(x1, p_ref[0, 1, 0])
    dim_ref[...] = _dot(x0, p_ref[0, 0, 1]) + _dot(x1, p_ref[0, 1, 1])


def _s5_state(xg, ppair, tr):
    g, r, _ = xg.shape
    sw = (g // 2) * LANES
    out = jax.ShapeDtypeStruct((r, sw), F32)
    ospec = pl.BlockSpec((tr, LANES), lambda i, j: (j, i))
    return pl.pallas_call(
        _s5_state_kernel,
        grid=(g // 2, r // tr),
        in_specs=[
            pl.BlockSpec((2, tr, S5_XW), lambda i, j: (i, j, 0)),
            pl.BlockSpec((1, 2, 2, S5_XW, LANES), lambda i, j: (i, 0, 0, 0, 0)),
        ],
        out_specs=[ospec, ospec],
        out_shape=[out, out],
        compiler_params=_cparams(("arbitrary", "arbitrary")),
        name="s5_state",
    )(xg, ppair)


def _s5_scan_kernel(dre_ref, dim_ref, are_ref, aim_ref, sre_ref, sim_ref, cre_ref, cim_ref, *, cb):
    @pl.when(pl.program_id(0) == 0)
    def _():
        cre_ref[...] = jnp.zeros_like(cre_ref)
        cim_ref[...] = jnp.zeros_like(cim_ref)

    are = are_ref[...]
    aim = aim_ref[...]

    def body(c, carry):
        s_re, s_im = carry
        sre_ref[c] = s_re
        sim_ref[c] = s_im
        n_re = are * s_re - aim * s_im + dre_ref[c]
        n_im = are * s_im + aim * s_re + dim_ref[c]
        return n_re, n_im

    s_re, s_im = lax.fori_loop(0, cb, body, (cre_ref[...], cim_ref[...]))
    cre_ref[...] = s_re
    cim_ref[...] = s_im


def _s5_scan(dre, dim, are, aim, cb):
    c, b, sw = dre.shape
    spec = pl.BlockSpec((cb, b, sw), lambda i: (i, 0, 0))
    out = jax.ShapeDtypeStruct((c, b, sw), F32)
    return pl.pallas_call(
        functools.partial(_s5_scan_kernel, cb=cb),
        grid=(c // cb,),
        in_specs=[spec, spec, _const_spec((1, sw)), _const_spec((1, sw))],
        out_specs=[spec, spec],
        out_shape=[out, out],
        scratch_shapes=[pltpu.VMEM((b, sw), F32), pltpu.VMEM((b, sw), F32)],
        compiler_params=_cparams(("arbitrary",)),
        name="s5_scan",
    )(dre, dim, are, aim)


def _gelu_tanh(y):
    return 0.5 * y * (1.0 + jnp.tanh(math.sqrt(2.0 / math.pi) * (y + 0.044715 * (y * y * y))))


def _s5_out_kernel(x_ref, sre_ref, sim_ref, m_ref, q_ref, y_ref):
    sre = sre_ref[...].astype(BF16)
    sim = sim_ref[...].astype(BF16)
    for gi in range(2):
        y = _dot(x_ref[gi], m_ref[gi]) + _dot(sre, q_ref[0, gi, 0]) + _dot(sim, q_ref[0, gi, 1])
        y_ref[gi] = _gelu_tanh(y).astype(BF16)


def _s5_out(xg, sre, sim, mg, qpair, tr):
    g, r, _ = xg.shape
    sspec = pl.BlockSpec((tr, LANES), lambda i, j: (j, i))
    xspec = pl.BlockSpec((2, tr, S5_XW), lambda i, j: (i, j, 0))
    return pl.pallas_call(
        _s5_out_kernel,
        grid=(g // 2, r // tr),
        in_specs=[
            xspec, sspec, sspec,
            pl.BlockSpec((2, S5_XW, S5_XW), lambda i, j: (i, 0, 0)),
            pl.BlockSpec((1, 2, 2, LANES, S5_XW), lambda i, j: (i, 0, 0, 0, 0)),
        ],
        out_specs=xspec,
        out_shape=jax.ShapeDtypeStruct((g, r, S5_XW), BF16),
        compiler_params=_cparams(("arbitrary", "arbitrary")),
        name="s5_out",
    )(xg, sre, sim, mg, qpair)


def _s5_params(lam_re, lam_im, b_re, b_im, c_re, c_im, d, log_dt):
    g = S5_GROUPS
    lre = jnp.minimum(lam_re.astype(F32), -1e-4)
    lim = lam_im.astype(F32)
    step = jnp.exp(log_dt.astype(F32))[:, None]
    pw = jnp.arange(S5_T + 1, dtype=F32)[:, None, None]
    mag = jnp.exp(pw * (lre * step)[None])
    ang = pw * (lim * step)[None]
    pre, pim = mag * jnp.cos(ang), mag * jnp.sin(ang)
    nr, ni = pre[1] - 1.0, pim[1]
    den = lre * lre + lim * lim
    cr = (nr * lre + ni * lim) / den
    ci = (ni * lre - nr * lim) / den
    bre = cr[..., None] * b_re - ci[..., None] * b_im
    bim = cr[..., None] * b_im + ci[..., None] * b_re
    lbr = pre[..., None] * bre[None] - pim[..., None] * bim[None]
    lbi = pre[..., None] * bim[None] + pim[..., None] * bre[None]
    kd = (jnp.einsum("gip,dgpj->dgij", c_re, lbr, precision=HIGHEST)
          - jnp.einsum("gip,dgpj->dgij", c_im, lbi, precision=HIGHEST))
    lag = np.arange(S5_T)[None, :] - np.arange(S5_T)[:, None]
    kst = kd[np.clip(lag, 0, S5_T)]
    kst = jnp.where((lag >= 0)[:, :, None, None, None], kst, 0.0)
    m = kst.transpose(2, 0, 4, 1, 3)
    eye_t = jnp.eye(S5_T, dtype=F32)[None, :, None, :, None]
    eye_i = jnp.eye(S5_GROUP, dtype=F32)[None, None, :, None, :]
    m = m + eye_t * eye_i * d.astype(F32)[:, None, None, None, :]
    m = m.reshape(g, S5_XW, S5_XW)
    rev = np.arange(S5_T - 1, -1, -1)
    p_re = lbr[rev].transpose(1, 0, 3, 2).reshape(g, S5_XW, S5_STATE)
    p_im = lbi[rev].transpose(1, 0, 3, 2).reshape(g, S5_XW, S5_STATE)
    qr = c_re[None] * pre[1:, :, None, :] - c_im[None] * pim[1:, :, None, :]
    qi = -(c_re[None] * pim[1:, :, None, :] + c_im[None] * pre[1:, :, None, :])
    q_re = qr.transpose(1, 3, 0, 2).reshape(g, S5_STATE, S5_XW)
    q_im = qi.transpose(1, 3, 0, 2).reshape(g, S5_STATE, S5_XW)

    zs = jnp.zeros_like(p_re)
    pa = jnp.stack([jnp.concatenate([p_re, zs], -1), jnp.concatenate([p_im, zs], -1)], 1)
    pb = jnp.stack([jnp.concatenate([zs, p_re], -1), jnp.concatenate([zs, p_im], -1)], 1)
    par = (np.arange(g) % 2 == 0)[:, None, None, None]
    ppair = jnp.where(par, pa, pb).reshape(g // 2, 2, 2, S5_XW, LANES)
    zq = jnp.zeros_like(q_re)
    qa = jnp.stack([jnp.concatenate([q_re, zq], 1), jnp.concatenate([q_im, zq], 1)], 1)
    qb = jnp.stack([jnp.concatenate([zq, q_re], 1), jnp.concatenate([zq, q_im], 1)], 1)
    qpair = jnp.where(par, qa, qb).reshape(g // 2, 2, 2, LANES, S5_XW)
    are = pre[S5_T].reshape(1, g * S5_STATE)
    aim = pim[S5_T].reshape(1, g * S5_STATE)
    return m.astype(BF16), ppair.astype(BF16), qpair.astype(BF16), are, aim


def _s5(su, sp, b, l):
    mg, ppair, qpair, are, aim = sp
    c = l // S5_T
    xg = su.reshape(b, c, S5_T, S5_GROUPS, S5_GROUP).transpose(3, 1, 0, 2, 4).reshape(S5_GROUPS, c * b, S5_XW)
    tr = min(1024, c * b)
    dre, dim = _s5_state(xg, ppair, tr)
    sw = dre.shape[-1]
    sre, sim = _s5_scan(dre.reshape(c, b, sw), dim.reshape(c, b, sw), are, aim, min(64, c))
    yg = _s5_out(xg, sre.reshape(c * b, sw), sim.reshape(c * b, sw), mg, qpair, tr)
    y = yg.reshape(S5_GROUPS, c, b, S5_T, S5_GROUP).transpose(2, 1, 3, 0, 4).reshape(b * l, S5_CH)
    return y


def _merge_kernel(x_ref, oa_ref, ob_ref, y_ref, g_ref, wa_ref, wb_ref, wc_ref, wglu_ref, bglu_ref, wout_ref,
                  o_ref):
    y = y_ref[...]
    oc = (y.astype(F32) * jax.nn.sigmoid(_dot(y, wglu_ref[...]) + bglu_ref[...])).astype(BF16)
    merged = (g_ref[:, 0:D_MODEL].astype(F32) * _dot(oa_ref[...], wa_ref[...])
              + g_ref[:, D_MODEL:2 * D_MODEL].astype(F32) * _dot(ob_ref[...], wb_ref[...])
              + g_ref[:, 2 * D_MODEL:].astype(F32) * _dot(oc, wc_ref[...]))
    o_ref[...] = x_ref[...] + _dot(merged.astype(BF16), wout_ref[...])


def _merge(xf, oa, ob, y, gates, p, tn):
    n = xf.shape[0]
    row = lambda w: pl.BlockSpec((tn, w), lambda i: (i, 0))
    return pl.pallas_call(
        _merge_kernel,
        grid=(n // tn,),
        in_specs=[
            row(D_MODEL), row(BRANCH_W), row(BRANCH_W), row(BRANCH_W), row(GATE_W),
            _const_spec((BRANCH_W, D_MODEL)), _const_spec((BRANCH_W, D_MODEL)), _const_spec((BRANCH_W, D_MODEL)),
            _const_spec((S5_CH, S5_CH)), _const_spec((1, S5_CH)), _const_spec((D_MODEL, D_MODEL)),
        ],
        out_specs=row(D_MODEL),
        out_shape=jax.ShapeDtypeStruct((n, D_MODEL), F32),
        compiler_params=_cparams(("arbitrary",)),
        name="merge",
    )(xf, oa, ob, y, gates, p["w_br_mla"], p["w_br_gla"], p["w_br_s5"], p["w_glu"], p["b_glu"], p["w_out"])


def _ffn_kernel(x_ref, g_ref, w1_ref, w2_ref, o_ref, h_ref, *, chunk):
    x = x_ref[...]
    ms = jnp.mean(x * x, axis=-1, keepdims=True)
    h_ref[...] = (x * lax.rsqrt(ms + EPS) * g_ref[...]).astype(BF16)
    o_ref[...] = x
    for c0 in range(0, D_FF, chunk):
        a = jnp.maximum(_dot(h_ref[...], w1_ref[:, c0:c0 + chunk]), 0.0)
        o_ref[...] += _dot((a * a).astype(BF16), w2_ref[c0:c0 + chunk, :])


def _ffn(xf, g, w1, w2, tn):
    n = xf.shape[0]
    return pl.pallas_call(
        functools.partial(_ffn_kernel, chunk=512),
        grid=(n // tn,),
        in_specs=[
            pl.BlockSpec((tn, D_MODEL), lambda i: (i, 0)),
            _const_spec((1, D_MODEL)),
            _const_spec((D_MODEL, D_FF)),
            _const_spec((D_FF, D_MODEL)),
        ],
        out_specs=pl.BlockSpec((tn, D_MODEL), lambda i: (i, 0)),
        out_shape=jax.ShapeDtypeStruct((n, D_MODEL), F32),
        scratch_shapes=[pltpu.VMEM((tn, D_MODEL), BF16)],
        compiler_params=_cparams(("arbitrary",)),
        name="ffn",
    )(xf, g, w1, w2)


def _head_slots(w, width):
    k = w.shape[0]
    w = w.reshape(k, MLA_HEADS, width)
    return jnp.pad(w, ((0, 0), (0, 0), (0, HEAD_SLOT - width))).reshape(k, MLA_HEADS * HEAD_SLOT)


def _swap_rope_halves(a):
    half = MLA_ROPE // 2
    return jnp.concatenate([jnp.zeros_like(a[..., :MLA_NOPE]), a[..., MLA_NOPE + half:], a[..., MLA_NOPE:MLA_NOPE + half]],
                           axis=-1)


def _constants(tg):
    c = {}
    bd = np.kron(np.eye(MLA_HEADS), np.ones((HEAD_SLOT, HEAD_SLOT)))
    c["bd_head"] = jnp.asarray(bd, BF16)
    t = np.arange(tg)
    same = (t[:, None] // GLA_CHUNK) == (t[None, :] // GLA_CHUNK)
    c["gla_tri"] = jnp.asarray(same & (t[None, :] <= t[:, None]), BF16)
    c["gla_upp"] = jnp.asarray(same & (t[None, :] > t[:, None]), BF16)
    hs = np.arange(GLA_HEADS * GLA_CHUNK) // GLA_CHUNK
    hk = np.arange(GLA_QW) // GLA_DK
    hv = np.arange(GLA_VW) // GLA_DV
    c["gla_mk"] = jnp.asarray(hs[:, None] == hk[None, :], BF16)
    c["gla_mv"] = jnp.asarray(hs[:, None] == hv[None, :], BF16)
    c["gla_mvt"] = jnp.asarray(hv[:, None] == hk[None, :], F32)
    s_in = np.arange(GLA_HEADS * GLA_CHUNK) % GLA_CHUNK
    c["gla_causal"] = jnp.asarray(s_in[None, :] <= np.arange(GLA_CHUNK)[:, None], F32)
    c["gla_bdn"] = jnp.asarray(np.kron(np.eye(GLA_HEADS), np.ones((GLA_DV, GLA_DV))), BF16)
    return c


def _rope_tables(l):
    pos = jnp.arange(l, dtype=F32)
    inv_freq = ROPE_BASE ** (-jnp.arange(0, MLA_ROPE, 2, dtype=F32) / MLA_ROPE)
    ang = pos[:, None] * inv_freq[None, :]
    cos, sin = jnp.cos(ang), jnp.sin(ang)
    pad = jnp.zeros((l, LANES - MLA_QK), F32)
    cos128 = jnp.concatenate([jnp.ones((l, MLA_NOPE), F32), cos, cos, pad], axis=-1)
    sin128 = jnp.concatenate([jnp.zeros((l, MLA_NOPE), F32), -sin, sin, pad], axis=-1)
    return cos128, sin128


def _layer_params(lyr, w_in, gate_b, mla_q_norm_g, mla_w_uq, mla_kv_norm_g, mla_w_ukv, mla_q_head_g,
                  mla_k_head_g, gla_w_gate, gla_b_gate, gla_out_g, s5_w_glu, s5_b_glu, w_br_mla, w_br_gla,
                  w_br_s5, w_out):
    p = {}
    w = w_in[lyr]
    sizes = (MLA_Q_RANK, MLA_KV_RANK, MLA_ROPE, GLA_QW, GLA_QW, GLA_VW, GLA_GATE_RANK, GLA_VW, S5_CH, GATE_W)
    offs = np.concatenate([[0], np.cumsum(sizes)])
    cq, ckv, kpe, gq, gk, gv, glr, gr, su, gates = [w[:, offs[i]:offs[i + 1]] for i in range(len(sizes))]
    half = MLA_ROPE // 2
    z = lambda n: jnp.zeros((D_MODEL, n), w.dtype)
    kpe_sw = jnp.concatenate([kpe[:, half:], kpe[:, :half]], axis=1)
    p["w_in"] = jnp.concatenate(
        [cq, ckv, z(MLA_NOPE), kpe, z(LANES - MLA_QK), z(MLA_NOPE), kpe_sw, z(LANES - MLA_QK),
         gq, gk, gv, gr, glr, z(LANES - GLA_GATE_RANK), su, gates], axis=1).astype(BF16)
    p["gate_b"] = gate_b[lyr].reshape(1, GATE_W)

    p["gqn"] = mla_q_norm_g[lyr].reshape(1, MLA_Q_RANK)
    p["gkvn"] = mla_kv_norm_g[lyr].reshape(1, MLA_KV_RANK)
    wuq = mla_w_uq[lyr].reshape(MLA_Q_RANK, MLA_HEADS, MLA_QK)
    p["wq"] = _head_slots(wuq.reshape(MLA_Q_RANK, -1), MLA_QK).astype(BF16)
    p["wqs"] = _head_slots(_swap_rope_halves(wuq).reshape(MLA_Q_RANK, -1), MLA_QK).astype(BF16)
    wukv = mla_w_ukv[lyr].reshape(MLA_KV_RANK, MLA_HEADS, MLA_NOPE + MLA_V)
    p["wk"] = _head_slots(wukv[..., :MLA_NOPE].reshape(MLA_KV_RANK, -1), MLA_NOPE).astype(BF16)
    wv = wukv[..., MLA_NOPE:]
    zv = jnp.zeros_like(wv)
    even = (np.arange(MLA_HEADS) % 2 == 0)[None, :, None]
    p["wv"] = jnp.where(even, jnp.concatenate([wv, zv], -1), jnp.concatenate([zv, wv], -1)).reshape(
        MLA_KV_RANK, MLA_HEADS * HEAD_SLOT).astype(BF16)
    vone = np.zeros((MLA_HEADS, HEAD_SLOT), np.float32)
    vone[0::2, MLA_V] = 1.0
    vone[1::2, 0] = 1.0
    p["vone"] = jnp.asarray(vone.reshape(1, -1))
    pad = lambda g: jnp.pad(g, (0, LANES - MLA_QK)).reshape(1, LANES)
    p["gq"] = pad(mla_q_head_g[lyr])
    p["gqs"] = pad(_swap_rope_halves(mla_q_head_g[lyr]))
    p["gk"] = pad(mla_k_head_g[lyr])
    p["gks"] = pad(_swap_rope_halves(mla_k_head_g[lyr]))

    p["w_gate"] = jnp.pad(gla_w_gate[lyr], ((0, LANES - GLA_GATE_RANK), (0, 0))).astype(BF16)
    p["b_gate"] = gla_b_gate[lyr].reshape(1, GLA_QW)
    p["gla_og"] = jnp.tile(gla_out_g[lyr], GLA_HEADS).reshape(1, GLA_VW)

    p["w_glu"] = s5_w_glu[lyr].astype(BF16)
    p["b_glu"] = s5_b_glu[lyr].reshape(1, S5_CH)
    p["w_br_mla"] = w_br_mla[lyr].astype(BF16)
    p["w_br_gla"] = w_br_gla[lyr].astype(BF16)
    p["w_br_s5"] = w_br_s5[lyr].astype(BF16)
    p["w_out"] = w_out[lyr].astype(BF16)
    return p


def kernel(x, norm1_g, w_in, mla_q_norm_g, mla_w_uq, mla_kv_norm_g, mla_w_ukv, mla_q_head_g, mla_k_head_g,
           gla_w_gate, gla_b_gate, gla_out_g, s5_lam_re, s5_lam_im, s5_b_re, s5_b_im, s5_c_re, s5_c_im, s5_d,
           s5_log_dt, s5_w_glu, s5_b_glu, w_br_mla, w_br_gla, w_br_s5, gate_b, w_out, norm2_g, w_ff1, w_ff2):
    b, l, d = x.shape
    n = b * l
    depth = w_in.shape[0]
    tn = min(512, n)
    tl = min(512, l)
    tg = min(256, l)
    cos128, sin128 = _rope_tables(l)
    consts = _constants(tg)
    xf = x.reshape(n, d)
    for lyr in range(depth):
        p = _layer_params(lyr, w_in, gate_b, mla_q_norm_g, mla_w_uq, mla_kv_norm_g, mla_w_ukv, mla_q_head_g,
                          mla_k_head_g, gla_w_gate, gla_b_gate, gla_out_g, s5_w_glu, s5_b_glu, w_br_mla,
                          w_br_gla, w_br_s5, w_out)
        p["bd_head"] = consts["bd_head"]
        sp = _s5_params(s5_lam_re[lyr], s5_lam_im[lyr], s5_b_re[lyr], s5_b_im[lyr], s5_c_re[lyr], s5_c_im[lyr],
                        s5_d[lyr], s5_log_dt[lyr])
        zmla, zgla, su, gates = _in_proj(xf, norm1_g[lyr].reshape(1, d), p["w_in"], p["gate_b"], tn)
        q, k, v = _mla_prep(zmla.reshape(b, l, ZMLA_W), cos128, sin128, p, tl)
        oa = _flash(q, k, v, tl).reshape(n, BRANCH_W)
        ob = _gla(zgla.reshape(b, l, ZGLA_W), p, consts, tg).reshape(n, BRANCH_W)
        y = _s5(su, sp, b, l)
        x1 = _merge(xf, oa, ob, y, gates, p, tn)
        xf = _ffn(x1, norm2_g[lyr].reshape(1, d), w_ff1[lyr].astype(BF16), w_ff2[lyr].astype(BF16), tn)
    return xf.reshape(b, l, d)
```

```python
import functools
import math

import numpy as np
import jax
import jax.numpy as jnp
from jax import lax
from jax.experimental import pallas as pl
from jax.experimental.pallas import tpu as pltpu

F32 = jnp.float32
BF16 = jnp.bfloat16
HIGHEST = lax.Precision.HIGHEST

D_MODEL = 1024
MLA_HEADS = 8
MLA_NOPE = 64
MLA_ROPE = 32
MLA_QK = MLA_NOPE + MLA_ROPE
MLA_V = 64
MLA_Q_RANK = 384
MLA_KV_RANK = 256
ROPE_BASE = 10000.0
GLA_HEADS = 4
GLA_DK = 64
GLA_DV = 128
GLA_GATE_RANK = 16
GLA_TAU = 16.0
GLA_CHUNK = 64
S5_CH = 512
S5_GROUP = 16
S5_GROUPS = S5_CH // S5_GROUP
S5_STATE = 64
N_BRANCH = 3
BRANCH_W = 512
D_FF = 4 * D_MODEL
EPS = 1e-6

LANES = 128
HEAD_SLOT = LANES
S5_T = 16
S5_XW = S5_T * S5_GROUP
VMEM_LIMIT = 56 * 1024 * 1024
NEG = -1e30
FLASH_TQ = 1024

ZMLA_W = MLA_Q_RANK + MLA_KV_RANK + 2 * LANES
ZGLA_W = 2 * GLA_HEADS * GLA_DK + 2 * GLA_HEADS * GLA_DV + LANES
GATE_W = N_BRANCH * D_MODEL


def _cparams(sem):
    return pltpu.CompilerParams(dimension_semantics=sem, vmem_limit_bytes=VMEM_LIMIT)


def _const_spec(shape):
    nd = len(shape)
    return pl.BlockSpec(shape, lambda *_: (0,) * nd)


def _dot(a, b):
    return jnp.dot(a, b, preferred_element_type=F32)


def _dot_nt(a, b):
    return lax.dot_general(a, b, (((1,), (1,)), ((), ())), preferred_element_type=F32)


def _in_proj_kernel(x_ref, g_ref, w_ref, gb_ref, zmla_ref, zgla_ref, su_ref, gate_ref, h_ref, *, chunk):
    x = x_ref[...]
    ms = jnp.mean(x * x, axis=-1, keepdims=True)
    h_ref[...] = (x * lax.rsqrt(ms + EPS) * g_ref[...]).astype(BF16)
    col = 0
    for out_ref in (zmla_ref, zgla_ref, su_ref):
        width = out_ref.shape[-1]
        for c0 in range(0, width, chunk):
            c1 = min(c0 + chunk, width)
            out_ref[:, c0:c1] = _dot(h_ref[...], w_ref[:, col + c0:col + c1]).astype(BF16)
        col += width
    for c0 in range(0, GATE_W, chunk):
        pre = _dot(h_ref[...], w_ref[:, col + c0:col + c0 + chunk]) + gb_ref[:, c0:c0 + chunk]
        gate_ref[:, c0:c0 + chunk] = jax.nn.sigmoid(pre).astype(BF16)


def _in_proj(xf, g, w, gate_b, tn):
    n = xf.shape[0]
    wtot = w.shape[1]
    return pl.pallas_call(
        functools.partial(_in_proj_kernel, chunk=512),
        grid=(n // tn,),
        in_specs=[
            pl.BlockSpec((tn, D_MODEL), lambda i: (i, 0)),
            _const_spec((1, D_MODEL)),
            _const_spec((D_MODEL, wtot)),
            _const_spec((1, GATE_W)),
        ],
        out_specs=[
            pl.BlockSpec((tn, ZMLA_W), lambda i: (i, 0)),
            pl.BlockSpec((tn, ZGLA_W), lambda i: (i, 0)),
            pl.BlockSpec((tn, S5_CH), lambda i: (i, 0)),
            pl.BlockSpec((tn, GATE_W), lambda i: (i, 0)),
        ],
        out_shape=[
            jax.ShapeDtypeStruct((n, ZMLA_W), BF16),
            jax.ShapeDtypeStruct((n, ZGLA_W), BF16),
            jax.ShapeDtypeStruct((n, S5_CH), BF16),
            jax.ShapeDtypeStruct((n, GATE_W), BF16),
        ],
        scratch_shapes=[pltpu.VMEM((tn, D_MODEL), BF16)],
        compiler_params=_cparams(("arbitrary",)),
        name="in_proj",
    )(xf, g, w, gate_b)


def _mla_prep_kernel(z_ref, cos_ref, sin_ref, gqn_ref, gkvn_ref, wq_ref, wqs_ref, wk_ref, wv_ref,
                     bd_ref, gq_ref, gqs_ref, gk_ref, gks_ref, vone_ref, q_ref, k_ref, v_ref):
    cq = z_ref[0, :, 0:MLA_Q_RANK].astype(F32)
    ckv = z_ref[0, :, MLA_Q_RANK:MLA_Q_RANK + MLA_KV_RANK].astype(F32)
    o = MLA_Q_RANK + MLA_KV_RANK
    kpe = z_ref[0, :, o:o + LANES].astype(F32)
    kpe_sw = z_ref[0, :, o + LANES:o + 2 * LANES].astype(F32)
    cos = cos_ref[...]
    sin = sin_ref[...]

    cqn = (cq * lax.rsqrt(jnp.mean(cq * cq, axis=-1, keepdims=True) + EPS) * gqn_ref[...]).astype(BF16)
    ckvn = (ckv * lax.rsqrt(jnp.mean(ckv * ckv, axis=-1, keepdims=True) + EPS) * gkvn_ref[...]).astype(BF16)

    q_raw = _dot(cqn, wq_ref[...])
    q_sw = _dot(cqn, wqs_ref[...])
    k_nope = _dot(ckvn, wk_ref[...])
    v_all = _dot(ckvn, wv_ref[...]) + vone_ref[...]

    bd = bd_ref[...]
    ssq_q = _dot((q_raw * q_raw).astype(BF16), bd)
    ssq_kn = _dot((k_nope * k_nope).astype(BF16), bd)
    ssq_pe = _dot((kpe * kpe).astype(BF16), bd[0:LANES, 0:LANES])
    rq = lax.rsqrt(ssq_q * (1.0 / MLA_QK) + EPS) * (MLA_QK ** -0.5 * math.log2(math.e))

    cq_t = gq_ref[...] * cos
    sq_t = gqs_ref[...] * sin
    ck_t = gk_ref[...] * cos
    kpe_rot = kpe * ck_t + kpe_sw * (gks_ref[...] * sin)
    for h in range(MLA_HEADS):
        sl = slice(h * HEAD_SLOT, (h + 1) * HEAD_SLOT)
        qh = rq[:, sl] * (q_raw[:, sl] * cq_t + q_sw[:, sl] * sq_t)
        q_ref[0, h] = qh.astype(BF16)
        rk = lax.rsqrt((ssq_kn[:, sl] + ssq_pe) * (1.0 / MLA_QK) + EPS)
        kh = rk * (k_nope[:, sl] * ck_t + kpe_rot)
        k_ref[0, h] = kh.astype(BF16)
        v_ref[0, h] = v_all[:, sl].astype(BF16)


def _mla_prep(zmla, cos128, sin128, p, tl):
    b, l, _ = zmla.shape
    hw = MLA_HEADS * HEAD_SLOT
    head_out = jax.ShapeDtypeStruct((b, MLA_HEADS, l, HEAD_SLOT), BF16)
    head_spec = pl.BlockSpec((1, MLA_HEADS, tl, HEAD_SLOT), lambda i, j: (i, 0, j, 0))
    return pl.pallas_call(
        _mla_prep_kernel,
        grid=(b, l // tl),
        in_specs=[
            pl.BlockSpec((1, tl, ZMLA_W), lambda i, j: (i, j, 0)),
            pl.BlockSpec((tl, LANES), lambda i, j: (j, 0)),
            pl.BlockSpec((tl, LANES), lambda i, j: (j, 0)),
            _const_spec((1, MLA_Q_RANK)),
            _const_spec((1, MLA_KV_RANK)),
            _const_spec((MLA_Q_RANK, hw)),
            _const_spec((MLA_Q_RANK, hw)),
            _const_spec((MLA_KV_RANK, hw)),
            _const_spec((MLA_KV_RANK, hw)),
            _const_spec((hw, hw)),
            _const_spec((1, LANES)),
            _const_spec((1, LANES)),
            _const_spec((1, LANES)),
            _const_spec((1, LANES)),
            _const_spec((1, hw)),
        ],
        out_specs=[head_spec, head_spec, head_spec],
        out_shape=[head_out, head_out, head_out],
        compiler_params=_cparams(("arbitrary", "arbitrary")),
        name="mla_prep",
    )(zmla, cos128, sin128, p["gqn"], p["gkvn"], p["wq"], p["wqs"], p["wk"], p["wv"], p["bd_head"],
      p["gq"], p["gqs"], p["gk"], p["gks"], p["vone"])


def _flash_kernel(q_ref, k_ref, v_ref, o_ref, *, tq):
    qi = pl.program_id(2)
    row = lax.broadcasted_iota(jnp.int32, (tq, tq), 0)
    col = lax.broadcasted_iota(jnp.int32, (tq, tq), 1)
    lane = lax.broadcasted_iota(jnp.int32, (tq, LANES), 1)

    def step(j, carry, masked):
        start = pl.multiple_of(j * tq, tq)
        new = []
        for hh in range(2):
            m, acc = carry[hh]
            k = k_ref[0, hh, pl.ds(start, tq), :]
            v = v_ref[0, hh, pl.ds(start, tq), :]
            s = _dot_nt(q_ref[0, hh], k)
            if masked:
                s = jnp.where(col <= row, s, NEG)
            m_new = jnp.maximum(m, jnp.max(s, axis=-1, keepdims=True))
            alpha = jnp.exp2(m - m_new)
            p = jnp.exp2(s - m_new)
            new.append((m_new, alpha * acc + _dot(p.astype(BF16), v)))
        return tuple(new)

    init = (jnp.full((tq, 1), NEG, F32), jnp.zeros((tq, LANES), F32))
    carry = lax.fori_loop(0, qi, functools.partial(step, masked=False), (init, init))
    (_, acc0), (_, acc1) = step(qi, carry, True)
    o0 = acc0 / acc0[:, MLA_V:MLA_V + 1]
    o1 = acc1 / acc1[:, 0:1]
    o_ref[0] = jnp.where(lane < MLA_V, o0, o1).astype(BF16)


def _flash(q, k, v, tq):
    b, h, l, _ = q.shape
    kv_spec = pl.BlockSpec((1, 2, l, HEAD_SLOT), lambda i, j, t: (i, j, 0, 0))
    return pl.pallas_call(
        functools.partial(_flash_kernel, tq=tq),
        grid=(b, h // 2, l // tq),
        in_specs=[pl.BlockSpec((1, 2, tq, HEAD_SLOT), lambda i, j, t: (i, j, t, 0)), kv_spec, kv_spec],
        out_specs=pl.BlockSpec((1, tq, LANES), lambda i, j, t: (i, t, j)),
        out_shape=jax.ShapeDtypeStruct((b, l, h * MLA_V), BF16),
        compiler_params=_cparams(("arbitrary", "arbitrary", "arbitrary")),
        name="flash",
    )(q, k, v)


GLA_QW = GLA_HEADS * GLA_DK
GLA_VW = GLA_HEADS * GLA_DV


def _split_bf16(a):
    hi = a.astype(BF16)
    lo = (a - hi.astype(F32)).astype(BF16)
    return hi, lo


def _gla_kernel(z_ref, wg_ref, bg_ref, og_ref, tri_ref, upp_ref, mk_ref, mv_ref, mvt_ref, causal_ref, bdn_ref,
                o_ref, st_ref, oacc_ref, *, tg):
    @pl.when(pl.program_id(1) == 0)
    def _():
        st_ref[...] = jnp.zeros_like(st_ref)

    q = z_ref[0, :, 0:GLA_QW].astype(F32)
    k = z_ref[0, :, GLA_QW:2 * GLA_QW].astype(F32)
    v = z_ref[0, :, 2 * GLA_QW:2 * GLA_QW + GLA_VW]
    r = z_ref[0, :, 2 * GLA_QW + GLA_VW:2 * GLA_QW + 2 * GLA_VW].astype(F32)
    glr = z_ref[0, :, 2 * GLA_QW + 2 * GLA_VW:]

    pre = _dot(glr, wg_ref[...]) + bg_ref[...]
    la = (jnp.minimum(pre, 0.0) - jnp.log(1.0 + jnp.exp(-jnp.abs(pre)))) * (1.0 / GLA_TAU)
    la_hi, la_lo = _split_bf16(la)
    bc = _dot(tri_ref[...], la_hi) + _dot(tri_ref[...], la_lo)
    rem = _dot(upp_ref[...], la_hi) + _dot(upp_ref[...], la_lo)
    qt = (q * (GLA_DK ** -0.5) * jnp.exp(bc)).astype(BF16)
    kt = (k * jnp.exp(-bc)).astype(BF16)
    kend = (k * jnp.exp(rem)).astype(BF16)

    for c in range(tg // GLA_CHUNK):
        sl = slice(c * GLA_CHUNK, (c + 1) * GLA_CHUNK)
        qc, kc, kec, vc = qt[sl], kt[sl], kend[sl], v[sl]
        krows = jnp.concatenate([kc] * GLA_HEADS, axis=0) * mk_ref[...]
        a = _dot_nt(qc, krows)
        a = jnp.where(causal_ref[...] > 0, a, 0.0).astype(BF16)
        vbd = jnp.concatenate([vc] * GLA_HEADS, axis=0) * mv_ref[...]
        st = st_ref[...]
        o = _dot(a, vbd) + _dot_nt(qc, st.astype(BF16))
        oacc_ref[sl, :] = o
        dst = _dot(vc.astype(F32).T.astype(BF16), kec)
        dec = jnp.exp(bc[(c + 1) * GLA_CHUNK - 1:(c + 1) * GLA_CHUNK, :])
        st_ref[...] = st * dec + dst * mvt_ref[...]

    o = oacc_ref[...]
    ss = _dot((o * o).astype(BF16), bdn_ref[...])
    y = o * lax.rsqrt(ss * (1.0 / GLA_DV) + EPS) * og_ref[...]
    o_ref[0] = (y * (r * jax.nn.sigmoid(r))).astype(BF16)


def _gla(zgla, p, c, tg):
    b, l, _ = zgla.shape
    return pl.pallas_call(
        functools.partial(_gla_kernel, tg=tg),
        grid=(b, l // tg),
        in_specs=[
            pl.BlockSpec((1, tg, ZGLA_W), lambda i, j: (i, j, 0)),
            _const_spec((LANES, GLA_QW)),
            _const_spec((1, GLA_QW)),
            _const_spec((1, GLA_VW)),
            _const_spec((tg, tg)),
            _const_spec((tg, tg)),
            _const_spec((GLA_HEADS * GLA_CHUNK, GLA_QW)),
            _const_spec((GLA_HEADS * GLA_CHUNK, GLA_VW)),
            _const_spec((GLA_VW, GLA_QW)),
            _const_spec((GLA_CHUNK, GLA_HEADS * GLA_CHUNK)),
            _const_spec((GLA_VW, GLA_VW)),
        ],
        out_specs=pl.BlockSpec((1, tg, GLA_VW), lambda i, j: (i, j, 0)),
        out_shape=jax.ShapeDtypeStruct((b, l, GLA_VW), BF16),
        scratch_shapes=[pltpu.VMEM((GLA_VW, GLA_QW), F32), pltpu.VMEM((tg, GLA_VW), F32)],
        compiler_params=_cparams(("arbitrary", "arbitrary")),
        name="gla",
    )(zgla, p["w_gate"], p["b_gate"], p["gla_og"], c["gla_tri"], c["gla_upp"], c["gla_mk"], c["gla_mv"],
      c["gla_mvt"], c["gla_causal"], c["gla_bdn"])


def _s5_state_kernel(x_ref, p_ref, dre_ref, dim_ref):
    x0 = x_ref[0]
    x1 = x_ref[1]
    dre_ref[...] = _dot(x0, p_ref[0, 0, 0]) + _dot(x1, p_ref[0, 1, 0])
    dim_ref[...] = _dot(x0, p_ref[0, 0, 1]) + _dot(x1, p_ref[0, 1, 1])


def _s5_state(xg, ppair, tr):
    g, r, _ = xg.shape
    sw = (g // 2) * LANES
    out = jax.ShapeDtypeStruct((r, sw), F32)
    ospec = pl.BlockSpec((tr, LANES), lambda i, j: (j, i))
    return pl.pallas_call(
        _s5_state_kernel,
        grid=(g // 2, r // tr),
        in_specs=[
            pl.BlockSpec((2, tr, S5_XW), lambda i, j: (i, j, 0)),
            pl.BlockSpec((1, 2, 2, S5_XW, LANES), lambda i, j: (i, 0, 0, 0, 0)),
        ],
        out_specs=[ospec, ospec],
        out_shape=[out, out],
        compiler_params=_cparams(("arbitrary", "arbitrary")),
        name="s5_state",
    )(xg, ppair)


def _s5_scan_kernel(dre_ref, dim_ref, are_ref, aim_ref, sre_ref, sim_ref, cre_ref, cim_ref, *, cb):
    @pl.when(pl.program_id(0) == 0)
    def _():
        cre_ref[...] = jnp.zeros_like(cre_ref)
        cim_ref[...] = jnp.zeros_like(cim_ref)

    are = are_ref[...]
    aim = aim_ref[...]

    def body(c, carry):
        s_re, s_im = carry
        sre_ref[c] = s_re
        sim_ref[c] = s_im
        n_re = are * s_re - aim * s_im + dre_ref[c]
        n_im = are * s_im + aim * s_re + dim_ref[c]
        return n_re, n_im

    s_re, s_im = lax.fori_loop(0, cb, body, (cre_ref[...], cim_ref[...]))
    cre_ref[...] = s_re
    cim_ref[...] = s_im


def _s5_scan(dre, dim, are, aim, cb):
    c, b, sw = dre.shape
    spec = pl.BlockSpec((cb, b, sw), lambda i: (i, 0, 0))
    out = jax.ShapeDtypeStruct((c, b, sw), F32)
    return pl.pallas_call(
        functools.partial(_s5_scan_kernel, cb=cb),
        grid=(c // cb,),
        in_specs=[spec, spec, _const_spec((1, sw)), _const_spec((1, sw))],
        out_specs=[spec, spec],
        out_shape=[out, out],
        scratch_shapes=[pltpu.VMEM((b, sw), F32), pltpu.VMEM((b, sw), F32)],
        compiler_params=_cparams(("arbitrary",)),
        name="s5_scan",
    )(dre, dim, are, aim)


def _gelu_tanh(y):
    return 0.5 * y * (1.0 + jnp.tanh(math.sqrt(2.0 / math.pi) * (y + 0.044715 * (y * y * y))))


def _s5_out_kernel(x_ref, sre_ref, sim_ref, m_ref, q_ref, y_ref):
    sre = sre_ref[...].astype(BF16)
    sim = sim_ref[...].astype(BF16)
    for gi in range(2):
        y = _dot(x_ref[gi], m_ref[gi]) + _dot(sre, q_ref[0, gi, 0]) + _dot(sim, q_ref[0, gi, 1])
        y_ref[gi] = _gelu_tanh(y).astype(BF16)


def _s5_out(xg, sre, sim, mg, qpair, tr):
    g, r, _ = xg.shape
    sspec = pl.BlockSpec((tr, LANES), lambda i, j: (j, i))
    xspec = pl.BlockSpec((2, tr, S5_XW), lambda i, j: (i, j, 0))
    return pl.pallas_call(
        _s5_out_kernel,
        grid=(g // 2, r // tr),
        in_specs=[
            xspec, sspec, sspec,
            pl.BlockSpec((2, S5_XW, S5_XW), lambda i, j: (i, 0, 0)),
            pl.BlockSpec((1, 2, 2, LANES, S5_XW), lambda i, j: (i, 0, 0, 0, 0)),
        ],
        out_specs=xspec,
        out_shape=jax.ShapeDtypeStruct((g, r, S5_XW), BF16),
        compiler_params=_cparams(("arbitrary", "arbitrary")),
        name="s5_out",
    )(xg, sre, sim, mg, qpair)


def _s5_params(lam_re, lam_im, b_re, b_im, c_re, c_im, d, log_dt):
    g = S5_GROUPS
    lre = jnp.minimum(lam_re.astype(F32), -1e-4)
    lim = lam_im.astype(F32)
    step = jnp.exp(log_dt.astype(F32))[:, None]
    pw = jnp.arange(S5_T + 1, dtype=F32)[:, None, None]
    mag = jnp.exp(pw * (lre * step)[None])
    ang = pw * (lim * step)[None]
    pre, pim = mag * jnp.cos(ang), mag * jnp.sin(ang)
    nr, ni = pre[1] - 1.0, pim[1]
    den = lre * lre + lim * lim
    cr = (nr * lre + ni * lim) / den
    ci = (ni * lre - nr * lim) / den
    bre = cr[..., None] * b_re - ci[..., None] * b_im
    bim = cr[..., None] * b_im + ci[..., None] * b_re
    lbr = pre[..., None] * bre[None] - pim[..., None] * bim[None]
    lbi = pre[..., None] * bim[None] + pim[..., None] * bre[None]
    kd = (jnp.einsum("gip,dgpj->dgij", c_re, lbr, precision=HIGHEST)
          - jnp.einsum("gip,dgpj->dgij", c_im, lbi, precision=HIGHEST))
    lag = np.arange(S5_T)[None, :] - np.arange(S5_T)[:, None]
    kst = kd[np.clip(lag, 0, S5_T)]
    kst = jnp.where((lag >= 0)[:, :, None, None, None], kst, 0.0)
    m = kst.transpose(2, 0, 4, 1, 3)
    eye_t = jnp.eye(S5_T, dtype=F32)[None, :, None, :, None]
    eye_i = jnp.eye(S5_GROUP, dtype=F32)[None, None, :, None, :]
    m = m + eye_t * eye_i * d.astype(F32)[:, None, None, None, :]
    m = m.reshape(g, S5_XW, S5_XW)
    rev = np.arange(S5_T - 1, -1, -1)
    p_re = lbr[rev].transpose(1, 0, 3, 2).reshape(g, S5_XW, S5_STATE)
    p_im = lbi[rev].transpose(1, 0, 3, 2).reshape(g, S5_XW, S5_STATE)
    qr = c_re[None] * pre[1:, :, None, :] - c_im[None] * pim[1:, :, None, :]
    qi = -(c_re[None] * pim[1:, :, None, :] + c_im[None] * pre[1:, :, None, :])
    q_re = qr.transpose(1, 3, 0, 2).reshape(g, S5_STATE, S5_XW)
    q_im = qi.transpose(1, 3, 0, 2).reshape(g, S5_STATE, S5_XW)

    zs = jnp.zeros_like(p_re)
    pa = jnp.stack([jnp.concatenate([p_re, zs], -1), jnp.concatenate([p_im, zs], -1)], 1)
    pb = jnp.stack([jnp.concatenate([zs, p_re], -1), jnp.concatenate([zs, p_im], -1)], 1)
    par = (np.arange(g) % 2 == 0)[:, None, None, None]
    ppair = jnp.where(par, pa, pb).reshape(g // 2, 2, 2, S5_XW, LANES)
    zq = jnp.zeros_like(q_re)
    qa = jnp.stack([jnp.concatenate([q_re, zq], 1), jnp.concatenate([q_im, zq], 1)], 1)
    qb = jnp.stack([jnp.concatenate([zq, q_re], 1), jnp.concatenate([zq, q_im], 1)], 1)
    qpair = jnp.where(par, qa, qb).reshape(g // 2, 2, 2, LANES, S5_XW)
    are = pre[S5_T].reshape(1, g * S5_STATE)
    aim = pim[S5_T].reshape(1, g * S5_STATE)
    return m.astype(BF16), ppair.astype(BF16), qpair.astype(BF16), are, aim


def _s5(su, sp, b, l):
    mg, ppair, qpair, are, aim = sp
    c = l // S5_T
    xg = su.reshape(b, c, S5_T, S5_GROUPS, S5_GROUP).transpose(3, 1, 0, 2, 4).reshape(S5_GROUPS, c * b, S5_XW)
    tr = min(1024, c * b)
    dre, dim = _s5_state(xg, ppair, tr)
    sw = dre.shape[-1]
    sre, sim = _s5_scan(dre.reshape(c, b, sw), dim.reshape(c, b, sw), are, aim, min(64, c))
    yg = _s5_out(xg, sre.reshape(c * b, sw), sim.reshape(c * b, sw), mg, qpair, tr)
    y = yg.reshape(S5_GROUPS, c, b, S5_T, S5_GROUP).transpose(2, 1, 3, 0, 4).reshape(b * l, S5_CH)
    return y


def _merge_kernel(x_ref, oa_ref, ob_ref, y_ref, g_ref, wa_ref, wb_ref, wc_ref, wglu_ref, bglu_ref, wout_ref,
                  o_ref):
    y = y_ref[...]
    oc = (y.astype(F32) * jax.nn.sigmoid(_dot(y, wglu_ref[...]) + bglu_ref[...])).astype(BF16)
    merged = (g_ref[:, 0:D_MODEL].astype(F32) * _dot(oa_ref[...], wa_ref[...])
              + g_ref[:, D_MODEL:2 * D_MODEL].astype(F32) * _dot(ob_ref[...], wb_ref[...])
              + g_ref[:, 2 * D_MODEL:].astype(F32) * _dot(oc, wc_ref[...]))
    o_ref[...] = x_ref[...] + _dot(merged.astype(BF16), wout_ref[...])


def _merge(xf, oa, ob, y, gates, p, tn):
    n = xf.shape[0]
    row = lambda w: pl.BlockSpec((tn, w), lambda i: (i, 0))
    return pl.pallas_call(
        _merge_kernel,
        grid=(n // tn,),
        in_specs=[
            row(D_MODEL), row(BRANCH_W), row(BRANCH_W), row(BRANCH_W), row(GATE_W),
            _const_spec((BRANCH_W, D_MODEL)), _const_spec((BRANCH_W, D_MODEL)), _const_spec((BRANCH_W, D_MODEL)),
            _const_spec((S5_CH, S5_CH)), _const_spec((1, S5_CH)), _const_spec((D_MODEL, D_MODEL)),
        ],
        out_specs=row(D_MODEL),
        out_shape=jax.ShapeDtypeStruct((n, D_MODEL), F32),
        compiler_params=_cparams(("arbitrary",)),
        name="merge",
    )(xf, oa, ob, y, gates, p["w_br_mla"], p["w_br_gla"], p["w_br_s5"], p["w_glu"], p["b_glu"], p["w_out"])


def _ffn_kernel(x_ref, g_ref, w1_ref, w2_ref, o_ref, h_ref, *, chunk):
    x = x_ref[...]
    ms = jnp.mean(x * x, axis=-1, keepdims=True)
    h_ref[...] = (x * lax.rsqrt(ms + EPS) * g_ref[...]).astype(BF16)
    o_ref[...] = x
    for c0 in range(0, D_FF, chunk):
        a = jnp.maximum(_dot(h_ref[...], w1_ref[:, c0:c0 + chunk]), 0.0)
        o_ref[...] += _dot((a * a).astype(BF16), w2_ref[c0:c0 + chunk, :])


def _ffn(xf, g, w1, w2, tn):
    n = xf.shape[0]
    return pl.pallas_call(
        functools.partial(_ffn_kernel, chunk=512),
        grid=(n // tn,),
        in_specs=[
            pl.BlockSpec((tn, D_MODEL), lambda i: (i, 0)),
            _const_spec((1, D_MODEL)),
            _const_spec((D_MODEL, D_FF)),
            _const_spec((D_FF, D_MODEL)),
        ],
        out_specs=pl.BlockSpec((tn, D_MODEL), lambda i: (i, 0)),
        out_shape=jax.ShapeDtypeStruct((n, D_MODEL), F32),
        scratch_shapes=[pltpu.VMEM((tn, D_MODEL), BF16)],
        compiler_params=_cparams(("arbitrary",)),
        name="ffn",
    )(xf, g, w1, w2)


def _head_slots(w, width):
    k = w.shape[0]
    w = w.reshape(k, MLA_HEADS, width)
    return jnp.pad(w, ((0, 0), (0, 0), (0, HEAD_SLOT - width))).reshape(k, MLA_HEADS * HEAD_SLOT)


def _swap_rope_halves(a):
    half = MLA_ROPE // 2
    return jnp.concatenate([jnp.zeros_like(a[..., :MLA_NOPE]), a[..., MLA_NOPE + half:], a[..., MLA_NOPE:MLA_NOPE + half]],
                           axis=-1)


def _constants(tg):
    c = {}
    bd = np.kron(np.eye(MLA_HEADS), np.ones((HEAD_SLOT, HEAD_SLOT)))
    c["bd_head"] = jnp.asarray(bd, BF16)
    t = np.arange(tg)
    same = (t[:, None] // GLA_CHUNK) == (t[None, :] // GLA_CHUNK)
    c["gla_tri"] = jnp.asarray(same & (t[None, :] <= t[:, None]), BF16)
    c["gla_upp"] = jnp.asarray(same & (t[None, :] > t[:, None]), BF16)
    hs = np.arange(GLA_HEADS * GLA_CHUNK) // GLA_CHUNK
    hk = np.arange(GLA_QW) // GLA_DK
    hv = np.arange(GLA_VW) // GLA_DV
    c["gla_mk"] = jnp.asarray(hs[:, None] == hk[None, :], BF16)
    c["gla_mv"] = jnp.asarray(hs[:, None] == hv[None, :], BF16)
    c["gla_mvt"] = jnp.asarray(hv[:, None] == hk[None, :], F32)
    s_in = np.arange(GLA_HEADS * GLA_CHUNK) % GLA_CHUNK
    c["gla_causal"] = jnp.asarray(s_in[None, :] <= np.arange(GLA_CHUNK)[:, None], F32)
    c["gla_bdn"] = jnp.asarray(np.kron(np.eye(GLA_HEADS), np.ones((GLA_DV, GLA_DV))), BF16)
    return c


def _rope_tables(l):
    pos = jnp.arange(l, dtype=F32)
    inv_freq = ROPE_BASE ** (-jnp.arange(0, MLA_ROPE, 2, dtype=F32) / MLA_ROPE)
    ang = pos[:, None] * inv_freq[None, :]
    cos, sin = jnp.cos(ang), jnp.sin(ang)
    pad = jnp.zeros((l, LANES - MLA_QK), F32)
    cos128 = jnp.concatenate([jnp.ones((l, MLA_NOPE), F32), cos, cos, pad], axis=-1)
    sin128 = jnp.concatenate([jnp.zeros((l, MLA_NOPE), F32), -sin, sin, pad], axis=-1)
    return cos128, sin128


def _layer_params(lyr, w_in, gate_b, mla_q_norm_g, mla_w_uq, mla_kv_norm_g, mla_w_ukv, mla_q_head_g,
                  mla_k_head_g, gla_w_gate, gla_b_gate, gla_out_g, s5_w_glu, s5_b_glu, w_br_mla, w_br_gla,
                  w_br_s5, w_out):
    p = {}
    w = w_in[lyr]
    sizes = (MLA_Q_RANK, MLA_KV_RANK, MLA_ROPE, GLA_QW, GLA_QW, GLA_VW, GLA_GATE_RANK, GLA_VW, S5_CH, GATE_W)
    offs = np.concatenate([[0], np.cumsum(sizes)])
    cq, ckv, kpe, gq, gk, gv, glr, gr, su, gates = [w[:, offs[i]:offs[i + 1]] for i in range(len(sizes))]
    half = MLA_ROPE // 2
    z = lambda n: jnp.zeros((D_MODEL, n), w.dtype)
    kpe_sw = jnp.concatenate([kpe[:, half:], kpe[:, :half]], axis=1)
    p["w_in"] = jnp.concatenate(
        [cq, ckv, z(MLA_NOPE), kpe, z(LANES - MLA_QK), z(MLA_NOPE), kpe_sw, z(LANES - MLA_QK),
         gq, gk, gv, gr, glr, z(LANES - GLA_GATE_RANK), su, gates], axis=1).astype(BF16)
    p["gate_b"] = gate_b[lyr].reshape(1, GATE_W)

    p["gqn"] = mla_q_norm_g[lyr].reshape(1, MLA_Q_RANK)
    p["gkvn"] = mla_kv_norm_g[lyr].reshape(1, MLA_KV_RANK)
    wuq = mla_w_uq[lyr].reshape(MLA_Q_RANK, MLA_HEADS, MLA_QK)
    p["wq"] = _head_slots(wuq.reshape(MLA_Q_RANK, -1), MLA_QK).astype(BF16)
    p["wqs"] = _head_slots(_swap_rope_halves(wuq).reshape(MLA_Q_RANK, -1), MLA_QK).astype(BF16)
    wukv = mla_w_ukv[lyr].reshape(MLA_KV_RANK, MLA_HEADS, MLA_NOPE + MLA_V)
    p["wk"] = _head_slots(wukv[..., :MLA_NOPE].reshape(MLA_KV_RANK, -1), MLA_NOPE).astype(BF16)
    wv = wukv[..., MLA_NOPE:]
    zv = jnp.zeros_like(wv)
    even = (np.arange(MLA_HEADS) % 2 == 0)[None, :, None]
    p["wv"] = jnp.where(even, jnp.concatenate([wv, zv], -1), jnp.concatenate([zv, wv], -1)).reshape(
        MLA_KV_RANK, MLA_HEADS * HEAD_SLOT).astype(BF16)
    vone = np.zeros((MLA_HEADS, HEAD_SLOT), np.float32)
    vone[0::2, MLA_V] = 1.0
    vone[1::2, 0] = 1.0
    p["vone"] = jnp.asarray(vone.reshape(1, -1))
    pad = lambda g: jnp.pad(g, (0, LANES - MLA_QK)).reshape(1, LANES)
    p["gq"] = pad(mla_q_head_g[lyr])
    p["gqs"] = pad(_swap_rope_halves(mla_q_head_g[lyr]))
    p["gk"] = pad(mla_k_head_g[lyr])
    p["gks"] = pad(_swap_rope_halves(mla_k_head_g[lyr]))

    p["w_gate"] = jnp.pad(gla_w_gate[lyr], ((0, LANES - GLA_GATE_RANK), (0, 0))).astype(BF16)
    p["b_gate"] = gla_b_gate[lyr].reshape(1, GLA_QW)
    p["gla_og"] = jnp.tile(gla_out_g[lyr], GLA_HEADS).reshape(1, GLA_VW)

    p["w_glu"] = s5_w_glu[lyr].astype(BF16)
    p["b_glu"] = s5_b_glu[lyr].reshape(1, S5_CH)
    p["w_br_mla"] = w_br_mla[lyr].astype(BF16)
    p["w_br_gla"] = w_br_gla[lyr].astype(BF16)
    p["w_br_s5"] = w_br_s5[lyr].astype(BF16)
    p["w_out"] = w_out[lyr].astype(BF16)
    return p


def kernel(x, norm1_g, w_in, mla_q_norm_g, mla_w_uq, mla_kv_norm_g, mla_w_ukv, mla_q_head_g, mla_k_head_g,
           gla_w_gate, gla_b_gate, gla_out_g, s5_lam_re, s5_lam_im, s5_b_re, s5_b_im, s5_c_re, s5_c_im, s5_d,
           s5_log_dt, s5_w_glu, s5_b_glu, w_br_mla, w_br_gla, w_br_s5, gate_b, w_out, norm2_g, w_ff1, w_ff2):
    b, l, d = x.shape
    n = b * l
    depth = w_in.shape[0]
    tn = min(512, n)
    tl = min(512, l)
    tg = min(256, l)
    cos128, sin128 = _rope_tables(l)
    consts = _constants(tg)
    xf = x.reshape(n, d)
    for lyr in range(depth):
        p = _layer_params(lyr, w_in, gate_b, mla_q_norm_g, mla_w_uq, mla_kv_norm_g, mla_w_ukv, mla_q_head_g,
                          mla_k_head_g, gla_w_gate, gla_b_gate, gla_out_g, s5_w_glu, s5_b_glu, w_br_mla,
                          w_br_gla, w_br_s5, w_out)
        p["bd_head"] = consts["bd_head"]
        sp = _s5_params(s5_lam_re[lyr], s5_lam_im[lyr], s5_b_re[lyr], s5_b_im[lyr], s5_c_re[lyr], s5_c_im[lyr],
                        s5_d[lyr], s5_log_dt[lyr])
        zmla, zgla, su, gates = _in_proj(xf, norm1_g[lyr].reshape(1, d), p["w_in"], p["gate_b"], tn)
        q, k, v = _mla_prep(zmla.reshape(b, l, ZMLA_W), cos128, sin128, p, tl)
        oa = _flash(q, k, v, min(FLASH_TQ, l)).reshape(n, BRANCH_W)
        ob = _gla(zgla.reshape(b, l, ZGLA_W), p, consts, tg).reshape(n, BRANCH_W)
        y = _s5(su, sp, b, l)
        x1 = _merge(xf, oa, ob, y, gates, p, tn)
        xf = _ffn(x1, norm2_g[lyr].reshape(1, d), w_ff1[lyr].astype(BF16), w_ff2[lyr].astype(BF16), tn)
    return xf.reshape(b, l, d)
```

```python
import functools
import math

import numpy as np
import jax
import jax.numpy as jnp
from jax import lax
from jax.experimental import pallas as pl
from jax.experimental.pallas import tpu as pltpu

F32 = jnp.float32
BF16 = jnp.bfloat16
HIGHEST = lax.Precision.HIGHEST

D_MODEL = 1024
MLA_HEADS = 8
MLA_NOPE = 64
MLA_ROPE = 32
MLA_QK = MLA_NOPE + MLA_ROPE
MLA_V = 64
MLA_Q_RANK = 384
MLA_KV_RANK = 256
ROPE_BASE = 10000.0
GLA_HEADS = 4
GLA_DK = 64
GLA_DV = 128
GLA_GATE_RANK = 16
GLA_TAU = 16.0
GLA_CHUNK = 64
S5_CH = 512
S5_GROUP = 16
S5_GROUPS = S5_CH // S5_GROUP
S5_STATE = 64
N_BRANCH = 3
BRANCH_W = 512
D_FF = 4 * D_MODEL
EPS = 1e-6

LANES = 128
HEAD_SLOT = LANES
SUBLANES = 8
S5_T = 8
S5_GB = LANES // S5_GROUP
S5_NB = S5_GROUPS // S5_GB
S5_BW = S5_T * LANES
S5_SW = S5_GB * S5_STATE
VMEM_LIMIT = 56 * 1024 * 1024
NEG = -1e30
FLASH_TQ = 1024

ZMLA_W = MLA_Q_RANK + MLA_KV_RANK + 2 * LANES
ZGLA_W = 2 * GLA_HEADS * GLA_DK + 2 * GLA_HEADS * GLA_DV + LANES
GATE_W = N_BRANCH * D_MODEL


def _cparams(sem):
    return pltpu.CompilerParams(dimension_semantics=sem, vmem_limit_bytes=VMEM_LIMIT)


def _const_spec(shape):
    nd = len(shape)
    return pl.BlockSpec(shape, lambda *_: (0,) * nd)


def _dot(a, b):
    return jnp.dot(a, b, preferred_element_type=F32)


def _dot_nt(a, b):
    return lax.dot_general(a, b, (((1,), (1,)), ((), ())), preferred_element_type=F32)


def _in_proj_kernel(x_ref, g_ref, w_ref, gb_ref, zmla_ref, zgla_ref, su_ref, gate_ref, h_ref, s_ref, *, chunk):
    x = x_ref[...]
    ms = jnp.mean(x * x, axis=-1, keepdims=True)
    h_ref[...] = (x * lax.rsqrt(ms + EPS) * g_ref[...]).astype(BF16)
    col = 0
    for out_ref in (zmla_ref, zgla_ref):
        width = out_ref.shape[-1]
        for c0 in range(0, width, chunk):
            c1 = min(c0 + chunk, width)
            out_ref[:, c0:c1] = _dot(h_ref[...], w_ref[:, col + c0:col + c1]).astype(BF16)
        col += width
    su = _dot(h_ref[...], w_ref[:, col:col + S5_CH])
    col += S5_CH
    rows = s_ref.shape[1] // S5_T
    for b in range(S5_NB):
        s_ref[b] = su[:, b * LANES:(b + 1) * LANES]
        for t in range(S5_T):
            su_ref[b, :, t * LANES:(t + 1) * LANES] = s_ref[b, pl.ds(t, rows, stride=S5_T), :].astype(BF16)
    for c0 in range(0, GATE_W, chunk):
        pre = _dot(h_ref[...], w_ref[:, col + c0:col + c0 + chunk]) + gb_ref[:, c0:c0 + chunk]
        gate_ref[:, c0:c0 + chunk] = jax.nn.sigmoid(pre).astype(BF16)


def _in_proj(xf, g, w, gate_b, tn):
    n = xf.shape[0]
    wtot = w.shape[1]
    return pl.pallas_call(
        functools.partial(_in_proj_kernel, chunk=512),
        grid=(n // tn,),
        in_specs=[
            pl.BlockSpec((tn, D_MODEL), lambda i: (i, 0)),
            _const_spec((1, D_MODEL)),
            _const_spec((D_MODEL, wtot)),
            _const_spec((1, GATE_W)),
        ],
        out_specs=[
            pl.BlockSpec((tn, ZMLA_W), lambda i: (i, 0)),
            pl.BlockSpec((tn, ZGLA_W), lambda i: (i, 0)),
            pl.BlockSpec((S5_NB, tn // S5_T, S5_BW), lambda i: (0, i, 0)),
            pl.BlockSpec((tn, GATE_W), lambda i: (i, 0)),
        ],
        out_shape=[
            jax.ShapeDtypeStruct((n, ZMLA_W), BF16),
            jax.ShapeDtypeStruct((n, ZGLA_W), BF16),
            jax.ShapeDtypeStruct((S5_NB, n // S5_T, S5_BW), BF16),
            jax.ShapeDtypeStruct((n, GATE_W), BF16),
        ],
        scratch_shapes=[pltpu.VMEM((tn, D_MODEL), BF16), pltpu.VMEM((S5_NB, tn, LANES), F32)],
        compiler_params=_cparams(("arbitrary",)),
        name="in_proj",
    )(xf, g, w, gate_b)


def _mla_prep_kernel(z_ref, cos_ref, sin_ref, gqn_ref, gkvn_ref, wq_ref, wqs_ref, wk_ref, wv_ref,
                     gq_ref, gqs_ref, gk_ref, gks_ref, vone_ref, q_ref, k_ref, v_ref):
    cq = z_ref[0, :, 0:MLA_Q_RANK].astype(F32)
    ckv = z_ref[0, :, MLA_Q_RANK:MLA_Q_RANK + MLA_KV_RANK].astype(F32)
    o = MLA_Q_RANK + MLA_KV_RANK
    kpe = z_ref[0, :, o:o + LANES].astype(F32)
    kpe_sw = z_ref[0, :, o + LANES:o + 2 * LANES].astype(F32)
    cos = cos_ref[...]
    sin = sin_ref[...]

    cqn = (cq * lax.rsqrt(jnp.mean(cq * cq, axis=-1, keepdims=True) + EPS) * gqn_ref[...]).astype(BF16)
    ckvn = (ckv * lax.rsqrt(jnp.mean(ckv * ckv, axis=-1, keepdims=True) + EPS) * gkvn_ref[...]).astype(BF16)

    q_raw = _dot(cqn, wq_ref[...])
    q_sw = _dot(cqn, wqs_ref[...])
    k_nope = _dot(ckvn, wk_ref[...])
    v_all = _dot(ckvn, wv_ref[...]) + vone_ref[...]

    ssq_pe = jnp.sum(kpe * kpe, axis=-1, keepdims=True)
    q_scale = MLA_QK ** -0.5 * math.log2(math.e)

    cq_t = gq_ref[...] * cos
    sq_t = gqs_ref[...] * sin
    ck_t = gk_ref[...] * cos
    kpe_rot = kpe * ck_t + kpe_sw * (gks_ref[...] * sin)
    for h in range(MLA_HEADS):
        sl = slice(h * HEAD_SLOT, (h + 1) * HEAD_SLOT)
        qr, kn = q_raw[:, sl], k_nope[:, sl]
        rq = lax.rsqrt(jnp.sum(qr * qr, axis=-1, keepdims=True) * (1.0 / MLA_QK) + EPS) * q_scale
        qh = rq * (qr * cq_t + q_sw[:, sl] * sq_t)
        q_ref[0, h] = qh.astype(BF16)
        rk = lax.rsqrt((jnp.sum(kn * kn, axis=-1, keepdims=True) + ssq_pe) * (1.0 / MLA_QK) + EPS)
        kh = rk * (kn * ck_t + kpe_rot)
        k_ref[0, h] = kh.astype(BF16)
        v_ref[0, h] = v_all[:, sl].astype(BF16)


def _mla_prep(zmla, cos128, sin128, p, tl):
    b, l, _ = zmla.shape
    hw = MLA_HEADS * HEAD_SLOT
    head_out = jax.ShapeDtypeStruct((b, MLA_HEADS, l, HEAD_SLOT), BF16)
    head_spec = pl.BlockSpec((1, MLA_HEADS, tl, HEAD_SLOT), lambda i, j: (i, 0, j, 0))
    return pl.pallas_call(
        _mla_prep_kernel,
        grid=(b, l // tl),
        in_specs=[
            pl.BlockSpec((1, tl, ZMLA_W), lambda i, j: (i, j, 0)),
            pl.BlockSpec((tl, LANES), lambda i, j: (j, 0)),
            pl.BlockSpec((tl, LANES), lambda i, j: (j, 0)),
            _const_spec((1, MLA_Q_RANK)),
            _const_spec((1, MLA_KV_RANK)),
            _const_spec((MLA_Q_RANK, hw)),
            _const_spec((MLA_Q_RANK, hw)),
            _const_spec((MLA_KV_RANK, hw)),
            _const_spec((MLA_KV_RANK, hw)),
            _const_spec((1, LANES)),
            _const_spec((1, LANES)),
            _const_spec((1, LANES)),
            _const_spec((1, LANES)),
            _const_spec((1, hw)),
        ],
        out_specs=[head_spec, head_spec, head_spec],
        out_shape=[head_out, head_out, head_out],
        compiler_params=_cparams(("arbitrary", "arbitrary")),
        name="mla_prep",
    )(zmla, cos128, sin128, p["gqn"], p["gkvn"], p["wq"], p["wqs"], p["wk"], p["wv"],
      p["gq"], p["gqs"], p["gk"], p["gks"], p["vone"])


def _flash_kernel(q_ref, k_ref, v_ref, o_ref, *, tq):
    qi = pl.program_id(2)
    row = lax.broadcasted_iota(jnp.int32, (tq, tq), 0)
    col = lax.broadcasted_iota(jnp.int32, (tq, tq), 1)
    lane = lax.broadcasted_iota(jnp.int32, (tq, LANES), 1)

    def step(j, carry, masked):
        start = pl.multiple_of(j * tq, tq)
        new = []
        for hh in range(2):
            m, acc = carry[hh]
            k = k_ref[0, hh, pl.ds(start, tq), :]
            v = v_ref[0, hh, pl.ds(start, tq), :]
            s = _dot_nt(q_ref[0, hh], k)
            if masked:
                s = jnp.where(col <= row, s, NEG)
            m_new = jnp.maximum(m, jnp.max(s, axis=-1, keepdims=True))
            alpha = jnp.exp2(m - m_new)
            p = jnp.exp2(s - m_new)
            new.append((m_new, alpha * acc + _dot(p.astype(BF16), v)))
        return tuple(new)

    init = (jnp.full((tq, 1), NEG, F32), jnp.zeros((tq, LANES), F32))
    carry = lax.fori_loop(0, qi, functools.partial(step, masked=False), (init, init))
    (_, acc0), (_, acc1) = step(qi, carry, True)
    o0 = acc0 / acc0[:, MLA_V:MLA_V + 1]
    o1 = acc1 / acc1[:, 0:1]
    o_ref[0] = jnp.where(lane < MLA_V, o0, o1).astype(BF16)


def _flash(q, k, v, tq):
    b, h, l, _ = q.shape
    kv_spec = pl.BlockSpec((1, 2, l, HEAD_SLOT), lambda i, j, t: (i, j, 0, 0))
    return pl.pallas_call(
        functools.partial(_flash_kernel, tq=tq),
        grid=(b, h // 2, l // tq),
        in_specs=[pl.BlockSpec((1, 2, tq, HEAD_SLOT), lambda i, j, t: (i, j, t, 0)), kv_spec, kv_spec],
        out_specs=pl.BlockSpec((1, tq, LANES), lambda i, j, t: (i, t, j)),
        out_shape=jax.ShapeDtypeStruct((b, l, h * MLA_V), BF16),
        compiler_params=_cparams(("arbitrary", "arbitrary", "arbitrary")),
        name="flash",
    )(q, k, v)


GLA_QW = GLA_HEADS * GLA_DK
GLA_VW = GLA_HEADS * GLA_DV


def _split_bf16(a):
    hi = a.astype(BF16)
    lo = (a - hi.astype(F32)).astype(BF16)
    return hi, lo


def _gla_kernel(z_ref, wg_ref, bg_ref, og_ref, tri_ref, upp_ref, mk_ref, mv_ref, mvt_ref, causal_ref, bdn_ref,
                o_ref, st_ref, oacc_ref, *, tg):
    @pl.when(pl.program_id(0) == 0)
    def _():
        st_ref[...] = jnp.zeros_like(st_ref)

    for bi in range(z_ref.shape[0]):
        _gla_block(z_ref.at[bi], wg_ref, bg_ref, og_ref, tri_ref, upp_ref, mk_ref, mv_ref, mvt_ref, causal_ref,
                   bdn_ref, o_ref.at[bi], st_ref.at[bi], oacc_ref.at[bi], tg)


def _gla_block(z_ref, wg_ref, bg_ref, og_ref, tri_ref, upp_ref, mk_ref, mv_ref, mvt_ref, causal_ref, bdn_ref,
               o_ref, st_ref, oacc_ref, tg):
    q = z_ref[:, 0:GLA_QW].astype(F32)
    k = z_ref[:, GLA_QW:2 * GLA_QW].astype(F32)
    v = z_ref[:, 2 * GLA_QW:2 * GLA_QW + GLA_VW]
    r = z_ref[:, 2 * GLA_QW + GLA_VW:2 * GLA_QW + 2 * GLA_VW].astype(F32)
    glr = z_ref[:, 2 * GLA_QW + 2 * GLA_VW:]

    pre = _dot(glr, wg_ref[...]) + bg_ref[...]
    la = (jnp.minimum(pre, 0.0) - jnp.log(1.0 + jnp.exp(-jnp.abs(pre)))) * (1.0 / GLA_TAU)
    la_hi, la_lo = _split_bf16(la)
    bc = _dot(tri_ref[...], la_hi) + _dot(tri_ref[...], la_lo)
    rem = _dot(upp_ref[...], la_hi) + _dot(upp_ref[...], la_lo)
    qt = (q * (GLA_DK ** -0.5) * jnp.exp(bc)).astype(BF16)
    kt = (k * jnp.exp(-bc)).astype(BF16)
    kend = (k * jnp.exp(rem)).astype(BF16)

    for c in range(tg // GLA_CHUNK):
        sl = slice(c * GLA_CHUNK, (c + 1) * GLA_CHUNK)
        qc, kc, kec, vc = qt[sl], kt[sl], kend[sl], v[sl]
        krows = jnp.concatenate([kc] * GLA_HEADS, axis=0) * mk_ref[...]
        a = _dot_nt(qc, krows)
        a = jnp.where(causal_ref[...] > 0, a, 0.0).astype(BF16)
        vbd = jnp.concatenate([vc] * GLA_HEADS, axis=0) * mv_ref[...]
        st = st_ref[...]
        o = _dot(a, vbd) + _dot_nt(qc, st.astype(BF16))
        oacc_ref[sl, :] = o
        dst = _dot(vc.astype(F32).T.astype(BF16), kec)
        dec = jnp.exp(bc[(c + 1) * GLA_CHUNK - 1:(c + 1) * GLA_CHUNK, :])
        st_ref[...] = st * dec + dst * mvt_ref[...]

    o = oacc_ref[...]
    ss = _dot((o * o).astype(BF16), bdn_ref[...])
    y = o * lax.rsqrt(ss * (1.0 / GLA_DV) + EPS) * og_ref[...]
    o_ref[...] = (y * (r * jax.nn.sigmoid(r))).astype(BF16)


def _gla(zgla, p, c, tg):
    b, l, _ = zgla.shape
    return pl.pallas_call(
        functools.partial(_gla_kernel, tg=tg),
        grid=(l // tg,),
        in_specs=[
            pl.BlockSpec((b, tg, ZGLA_W), lambda j: (0, j, 0)),
            _const_spec((LANES, GLA_QW)),
            _const_spec((1, GLA_QW)),
            _const_spec((1, GLA_VW)),
            _const_spec((tg, tg)),
            _const_spec((tg, tg)),
            _const_spec((GLA_HEADS * GLA_CHUNK, GLA_QW)),
            _const_spec((GLA_HEADS * GLA_CHUNK, GLA_VW)),
            _const_spec((GLA_VW, GLA_QW)),
            _const_spec((GLA_CHUNK, GLA_HEADS * GLA_CHUNK)),
            _const_spec((GLA_VW, GLA_VW)),
        ],
        out_specs=pl.BlockSpec((b, tg, GLA_VW), lambda j: (0, j, 0)),
        out_shape=jax.ShapeDtypeStruct((b, l, GLA_VW), BF16),
        scratch_shapes=[pltpu.VMEM((b, GLA_VW, GLA_QW), F32), pltpu.VMEM((b, tg, GLA_VW), F32)],
        compiler_params=_cparams(("arbitrary",)),
        name="gla",
    )(zgla, p["w_gate"], p["b_gate"], p["gla_og"], c["gla_tri"], c["gla_upp"], c["gla_mk"], c["gla_mv"],
      c["gla_mvt"], c["gla_causal"], c["gla_bdn"])


def _s5_state_kernel(x_ref, p_ref, dre_ref, dim_ref):
    d = _dot(x_ref[0], p_ref[0])
    dre_ref[...] = d[:, :S5_SW]
    dim_ref[...] = d[:, S5_SW:]


def _s5_state(xb, p8, tr):
    nb, r, _ = xb.shape
    out = jax.ShapeDtypeStruct((r, nb * S5_SW), F32)
    ospec = pl.BlockSpec((tr, S5_SW), lambda i, j: (j, i))
    return pl.pallas_call(
        _s5_state_kernel,
        grid=(nb, r // tr),
        in_specs=[
            pl.BlockSpec((1, tr, S5_BW), lambda i, j: (i, j, 0)),
            pl.BlockSpec((1, S5_BW, 2 * S5_SW), lambda i, j: (i, 0, 0)),
        ],
        out_specs=[ospec, ospec],
        out_shape=[out, out],
        compiler_params=_cparams(("arbitrary", "arbitrary")),
        name="s5_state",
    )(xb, p8)


def _s5_scan_kernel(dre_ref, dim_ref, are_ref, aim_ref, sre_ref, sim_ref, wre_ref, wim_ref, cre_ref, cim_ref,
                    *, cb, nbatch):
    @pl.when(pl.program_id(0) == 0)
    def _():
        wre_ref[...] = jnp.zeros_like(wre_ref)
        wim_ref[...] = jnp.zeros_like(wim_ref)
        cre_ref[...] = jnp.zeros_like(cre_ref)
        cim_ref[...] = jnp.zeros_like(cim_ref)

    nt = wre_ref.shape[0]
    for b in range(nbatch):
        for j in range(nt):
            wre_ref[j, pl.ds(b, cb, stride=SUBLANES), :] = dre_ref[b, :, j * LANES:(j + 1) * LANES]
            wim_ref[j, pl.ds(b, cb, stride=SUBLANES), :] = dim_ref[b, :, j * LANES:(j + 1) * LANES]
    are = jnp.broadcast_to(are_ref[...], cre_ref.shape)
    aim = jnp.broadcast_to(aim_ref[...], cre_ref.shape)

    def body(c, carry):
        s_re, s_im = carry
        rows = pl.ds(pl.multiple_of(c * SUBLANES, SUBLANES), SUBLANES)
        d_re = wre_ref[:, rows, :]
        d_im = wim_ref[:, rows, :]
        wre_ref[:, rows, :] = s_re
        wim_ref[:, rows, :] = s_im
        return are * s_re - aim * s_im + d_re, are * s_im + aim * s_re + d_im

    s_re, s_im = lax.fori_loop(0, cb, body, (cre_ref[...], cim_ref[...]))
    cre_ref[...] = s_re
    cim_ref[...] = s_im
    for b in range(nbatch):
        for j in range(nt):
            sre_ref[b, :, j * LANES:(j + 1) * LANES] = wre_ref[j, pl.ds(b, cb, stride=SUBLANES), :]
            sim_ref[b, :, j * LANES:(j + 1) * LANES] = wim_ref[j, pl.ds(b, cb, stride=SUBLANES), :]


def _s5_scan(dre, dim, are, aim, cb):
    b, c, sw = dre.shape
    nt = sw // LANES
    spec = pl.BlockSpec((b, cb, sw), lambda i: (0, i, 0))
    out = jax.ShapeDtypeStruct((b, c, sw), F32)
    work = pltpu.VMEM((nt, cb * SUBLANES, LANES), F32)
    carry = pltpu.VMEM((nt, SUBLANES, LANES), F32)
    return pl.pallas_call(
        functools.partial(_s5_scan_kernel, cb=cb, nbatch=b),
        grid=(c // cb,),
        in_specs=[spec, spec, _const_spec((nt, 1, LANES)), _const_spec((nt, 1, LANES))],
        out_specs=[spec, spec],
        out_shape=[out, out],
        scratch_shapes=[work, work, carry, carry],
        compiler_params=_cparams(("arbitrary",)),
        name="s5_scan",
    )(dre, dim, are.reshape(nt, 1, LANES), aim.reshape(nt, 1, LANES))


def _gelu_tanh(y):
    return 0.5 * y * (1.0 + jnp.tanh(math.sqrt(2.0 / math.pi) * (y + 0.044715 * (y * y * y))))


def _s5_out_kernel(x_ref, sre_ref, sim_ref, m_ref, q_ref, y_ref):
    s8 = jnp.concatenate([sre_ref[...], sim_ref[...]], axis=1).astype(BF16)
    y = _dot(x_ref[0], m_ref[0]) + _dot(s8, q_ref[0])
    y_ref[0] = _gelu_tanh(y).astype(BF16)


def _s5_out(xb, sre, sim, m8, q8, tr):
    nb, r, _ = xb.shape
    sspec = pl.BlockSpec((tr, S5_SW), lambda i, j: (j, i))
    xspec = pl.BlockSpec((1, tr, S5_BW), lambda i, j: (i, j, 0))
    return pl.pallas_call(
        _s5_out_kernel,
        grid=(nb, r // tr),
        in_specs=[
            xspec, sspec, sspec,
            pl.BlockSpec((1, S5_BW, S5_BW), lambda i, j: (i, 0, 0)),
            pl.BlockSpec((1, 2 * S5_SW, S5_BW), lambda i, j: (i, 0, 0)),
        ],
        out_specs=xspec,
        out_shape=jax.ShapeDtypeStruct((nb, r, S5_BW), BF16),
        compiler_params=_cparams(("arbitrary", "arbitrary")),
        name="s5_out",
    )(xb, sre, sim, m8, q8)


def _s5_params(lam_re, lam_im, b_re, b_im, c_re, c_im, d, log_dt):
    g = S5_GROUPS
    lre = jnp.minimum(lam_re.astype(F32), -1e-4)
    lim = lam_im.astype(F32)
    step = jnp.exp(log_dt.astype(F32))[:, None]
    pw = jnp.arange(S5_T + 1, dtype=F32)[:, None, None]
    mag = jnp.exp(pw * (lre * step)[None])
    ang = pw * (lim * step)[None]
    pre, pim = mag * jnp.cos(ang), mag * jnp.sin(ang)
    nr, ni = pre[1] - 1.0, pim[1]
    den = lre * lre + lim * lim
    cr = (nr * lre + ni * lim) / den
    ci = (ni * lre - nr * lim) / den
    bre = cr[..., None] * b_re - ci[..., None] * b_im
    bim = cr[..., None] * b_im + ci[..., None] * b_re
    lbr = pre[..., None] * bre[None] - pim[..., None] * bim[None]
    lbi = pre[..., None] * bim[None] + pim[..., None] * bre[None]
    kd = (jnp.einsum("gip,dgpj->dgij", c_re, lbr, precision=HIGHEST)
          - jnp.einsum("gip,dgpj->dgij", c_im, lbi, precision=HIGHEST))
    lag = np.arange(S5_T)[None, :] - np.arange(S5_T)[:, None]
    kst = kd[np.clip(lag, 0, S5_T)]
    kst = jnp.where((lag >= 0)[:, :, None, None, None], kst, 0.0)
    m = kst.transpose(2, 0, 4, 1, 3)
    eye_t = jnp.eye(S5_T, dtype=F32)[None, :, None, :, None]
    eye_i = jnp.eye(S5_GROUP, dtype=F32)[None, None, :, None, :]
    m = m + eye_t * eye_i * d.astype(F32)[:, None, None, None, :]
    rev = np.arange(S5_T - 1, -1, -1)
    p_re = lbr[rev].transpose(1, 0, 3, 2)
    p_im = lbi[rev].transpose(1, 0, 3, 2)
    qr = c_re[None] * pre[1:, :, None, :] - c_im[None] * pim[1:, :, None, :]
    qi = -(c_re[None] * pim[1:, :, None, :] + c_im[None] * pre[1:, :, None, :])
    q_re = qr.transpose(1, 3, 0, 2)
    q_im = qi.transpose(1, 3, 0, 2)

    nb, gb = S5_NB, S5_GB
    eye = jnp.eye(gb, dtype=F32)
    m8 = m.reshape(nb, gb, S5_T, S5_GROUP, S5_T, 1, S5_GROUP) * eye[None, :, None, None, None, :, None]
    m8 = m8.transpose(0, 2, 1, 3, 4, 5, 6).reshape(nb, S5_BW, S5_BW)

    def embed_p(p):
        p = p.reshape(nb, gb, S5_T, S5_GROUP, 1, S5_STATE) * eye[None, :, None, None, :, None]
        return p.transpose(0, 2, 1, 3, 4, 5).reshape(nb, S5_BW, S5_SW)

    def embed_q(q):
        q = q.reshape(nb, gb, S5_STATE, S5_T, 1, S5_GROUP) * eye[None, :, None, None, :, None]
        return q.reshape(nb, S5_SW, S5_BW)

    p8 = jnp.concatenate([embed_p(p_re), embed_p(p_im)], axis=2)
    q8 = jnp.concatenate([embed_q(q_re), embed_q(q_im)], axis=1)
    are = pre[S5_T].reshape(1, g * S5_STATE)
    aim = pim[S5_T].reshape(1, g * S5_STATE)
    return m8.astype(BF16), p8.astype(BF16), q8.astype(BF16), are, aim


def _s5(xb, sp, b, l):
    m8, p8, q8, are, aim = sp
    c = l // S5_T
    r = b * c
    tr = min(1024, r)
    dre, dim = _s5_state(xb, p8, tr)
    sw = dre.shape[-1]
    sre, sim = _s5_scan(dre.reshape(b, c, sw), dim.reshape(b, c, sw), are, aim, min(64, c))
    return _s5_out(xb, sre.reshape(r, sw), sim.reshape(r, sw), m8, q8, tr)


def _merge_kernel(x_ref, oa_ref, ob_ref, y_ref, g_ref, wa_ref, wb_ref, wc_ref, wglu_ref, bglu_ref, wout_ref,
                  o_ref, ys_ref):
    rows = y_ref.shape[1]
    for b in range(S5_NB):
        for t in range(S5_T):
            ys_ref[b, pl.ds(t, rows, stride=S5_T), :] = y_ref[b, :, t * LANES:(t + 1) * LANES].astype(F32)
    y = jnp.concatenate([ys_ref[b] for b in range(S5_NB)], axis=1)
    oc = (y * jax.nn.sigmoid(_dot(y.astype(BF16), wglu_ref[...]) + bglu_ref[...])).astype(BF16)
    merged = (g_ref[:, 0:D_MODEL].astype(F32) * _dot(oa_ref[...], wa_ref[...])
              + g_ref[:, D_MODEL:2 * D_MODEL].astype(F32) * _dot(ob_ref[...], wb_ref[...])
              + g_ref[:, 2 * D_MODEL:].astype(F32) * _dot(oc, wc_ref[...]))
    o_ref[...] = x_ref[...] + _dot(merged.astype(BF16), wout_ref[...])


def _merge(xf, oa, ob, y, gates, p, tn):
    n = xf.shape[0]
    row = lambda w: pl.BlockSpec((tn, w), lambda i: (i, 0))
    return pl.pallas_call(
        _merge_kernel,
        grid=(n // tn,),
        in_specs=[
            row(D_MODEL), row(BRANCH_W), row(BRANCH_W),
            pl.BlockSpec((S5_NB, tn // S5_T, S5_BW), lambda i: (0, i, 0)), row(GATE_W),
            _const_spec((BRANCH_W, D_MODEL)), _const_spec((BRANCH_W, D_MODEL)), _const_spec((BRANCH_W, D_MODEL)),
            _const_spec((S5_CH, S5_CH)), _const_spec((1, S5_CH)), _const_spec((D_MODEL, D_MODEL)),
        ],
        out_specs=row(D_MODEL),
        out_shape=jax.ShapeDtypeStruct((n, D_MODEL), F32),
        scratch_shapes=[pltpu.VMEM((S5_NB, tn, LANES), F32)],
        compiler_params=_cparams(("arbitrary",)),
        name="merge",
    )(xf, oa, ob, y, gates, p["w_br_mla"], p["w_br_gla"], p["w_br_s5"], p["w_glu"], p["b_glu"], p["w_out"])


def _ffn_kernel(x_ref, g_ref, w1_ref, w2_ref, o_ref, h_ref, *, chunk):
    x = x_ref[...]
    ms = jnp.mean(x * x, axis=-1, keepdims=True)
    h_ref[...] = (x * lax.rsqrt(ms + EPS) * g_ref[...]).astype(BF16)
    o_ref[...] = x
    for c0 in range(0, D_FF, chunk):
        a = jnp.maximum(_dot(h_ref[...], w1_ref[:, c0:c0 + chunk]), 0.0)
        o_ref[...] += _dot((a * a).astype(BF16), w2_ref[c0:c0 + chunk, :])


def _ffn(xf, g, w1, w2, tn):
    n = xf.shape[0]
    return pl.pallas_call(
        functools.partial(_ffn_kernel, chunk=512),
        grid=(n // tn,),
        in_specs=[
            pl.BlockSpec((tn, D_MODEL), lambda i: (i, 0)),
            _const_spec((1, D_MODEL)),
            _const_spec((D_MODEL, D_FF)),
            _const_spec((D_FF, D_MODEL)),
        ],
        out_specs=pl.BlockSpec((tn, D_MODEL), lambda i: (i, 0)),
        out_shape=jax.ShapeDtypeStruct((n, D_MODEL), F32),
        scratch_shapes=[pltpu.VMEM((tn, D_MODEL), BF16)],
        compiler_params=_cparams(("arbitrary",)),
        name="ffn",
    )(xf, g, w1, w2)


def _head_slots(w, width):
    k = w.shape[0]
    w = w.reshape(k, MLA_HEADS, width)
    return jnp.pad(w, ((0, 0), (0, 0), (0, HEAD_SLOT - width))).reshape(k, MLA_HEADS * HEAD_SLOT)


def _swap_rope_halves(a):
    half = MLA_ROPE // 2
    return jnp.concatenate([jnp.zeros_like(a[..., :MLA_NOPE]), a[..., MLA_NOPE + half:], a[..., MLA_NOPE:MLA_NOPE + half]],
                           axis=-1)


def _constants(tg):
    c = {}
    t = np.arange(tg)
    same = (t[:, None] // GLA_CHUNK) == (t[None, :] // GLA_CHUNK)
    c["gla_tri"] = jnp.asarray(same & (t[None, :] <= t[:, None]), BF16)
    c["gla_upp"] = jnp.asarray(same & (t[None, :] > t[:, None]), BF16)
    hs = np.arange(GLA_HEADS * GLA_CHUNK) // GLA_CHUNK
    hk = np.arange(GLA_QW) // GLA_DK
    hv = np.arange(GLA_VW) // GLA_DV
    c["gla_mk"] = jnp.asarray(hs[:, None] == hk[None, :], BF16)
    c["gla_mv"] = jnp.asarray(hs[:, None] == hv[None, :], BF16)
    c["gla_mvt"] = jnp.asarray(hv[:, None] == hk[None, :], F32)
    s_in = np.arange(GLA_HEADS * GLA_CHUNK) % GLA_CHUNK
    c["gla_causal"] = jnp.asarray(s_in[None, :] <= np.arange(GLA_CHUNK)[:, None], F32)
    c["gla_bdn"] = jnp.asarray(np.kron(np.eye(GLA_HEADS), np.ones((GLA_DV, GLA_DV))), BF16)
    return c


def _rope_tables(l):
    pos = jnp.arange(l, dtype=F32)
    inv_freq = ROPE_BASE ** (-jnp.arange(0, MLA_ROPE, 2, dtype=F32) / MLA_ROPE)
    ang = pos[:, None] * inv_freq[None, :]
    cos, sin = jnp.cos(ang), jnp.sin(ang)
    pad = jnp.zeros((l, LANES - MLA_QK), F32)
    cos128 = jnp.concatenate([jnp.ones((l, MLA_NOPE), F32), cos, cos, pad], axis=-1)
    sin128 = jnp.concatenate([jnp.zeros((l, MLA_NOPE), F32), -sin, sin, pad], axis=-1)
    return cos128, sin128


def _layer_params(lyr, w_in, gate_b, mla_q_norm_g, mla_w_uq, mla_kv_norm_g, mla_w_ukv, mla_q_head_g,
                  mla_k_head_g, gla_w_gate, gla_b_gate, gla_out_g, s5_w_glu, s5_b_glu, w_br_mla, w_br_gla,
                  w_br_s5, w_out):
    p = {}
    w = w_in[lyr]
    sizes = (MLA_Q_RANK, MLA_KV_RANK, MLA_ROPE, GLA_QW, GLA_QW, GLA_VW, GLA_GATE_RANK, GLA_VW, S5_CH, GATE_W)
    offs = np.concatenate([[0], np.cumsum(sizes)])
    cq, ckv, kpe, gq, gk, gv, glr, gr, su, gates = [w[:, offs[i]:offs[i + 1]] for i in range(len(sizes))]
    half = MLA_ROPE // 2
    z = lambda n: jnp.zeros((D_MODEL, n), w.dtype)
    kpe_sw = jnp.concatenate([kpe[:, half:], kpe[:, :half]], axis=1)
    p["w_in"] = jnp.concatenate(
        [cq, ckv, z(MLA_NOPE), kpe, z(LANES - MLA_QK), z(MLA_NOPE), kpe_sw, z(LANES - MLA_QK),
         gq, gk, gv, gr, glr, z(LANES - GLA_GATE_RANK), su, gates], axis=1).astype(BF16)
    p["gate_b"] = gate_b[lyr].reshape(1, GATE_W)

    p["gqn"] = mla_q_norm_g[lyr].reshape(1, MLA_Q_RANK)
    p["gkvn"] = mla_kv_norm_g[lyr].reshape(1, MLA_KV_RANK)
    wuq = mla_w_uq[lyr].reshape(MLA_Q_RANK, MLA_HEADS, MLA_QK)
    p["wq"] = _head_slots(wuq.reshape(MLA_Q_RANK, -1), MLA_QK).astype(BF16)
    p["wqs"] = _head_slots(_swap_rope_halves(wuq).reshape(MLA_Q_RANK, -1), MLA_QK).astype(BF16)
    wukv = mla_w_ukv[lyr].reshape(MLA_KV_RANK, MLA_HEADS, MLA_NOPE + MLA_V)
    p["wk"] = _head_slots(wukv[..., :MLA_NOPE].reshape(MLA_KV_RANK, -1), MLA_NOPE).astype(BF16)
    wv = wukv[..., MLA_NOPE:]
    zv = jnp.zeros_like(wv)
    even = (np.arange(MLA_HEADS) % 2 == 0)[None, :, None]
    p["wv"] = jnp.where(even, jnp.concatenate([wv, zv], -1), jnp.concatenate([zv, wv], -1)).reshape(
        MLA_KV_RANK, MLA_HEADS * HEAD_SLOT).astype(BF16)
    vone = np.zeros((MLA_HEADS, HEAD_SLOT), np.float32)
    vone[0::2, MLA_V] = 1.0
    vone[1::2, 0] = 1.0
    p["vone"] = jnp.asarray(vone.reshape(1, -1))
    pad = lambda g: jnp.pad(g, (0, LANES - MLA_QK)).reshape(1, LANES)
    p["gq"] = pad(mla_q_head_g[lyr])
    p["gqs"] = pad(_swap_rope_halves(mla_q_head_g[lyr]))
    p["gk"] = pad(mla_k_head_g[lyr])
    p["gks"] = pad(_swap_rope_halves(mla_k_head_g[lyr]))

    p["w_gate"] = jnp.pad(gla_w_gate[lyr], ((0, LANES - GLA_GATE_RANK), (0, 0))).astype(BF16)
    p["b_gate"] = gla_b_gate[lyr].reshape(1, GLA_QW)
    p["gla_og"] = jnp.tile(gla_out_g[lyr], GLA_HEADS).reshape(1, GLA_VW)

    p["w_glu"] = s5_w_glu[lyr].astype(BF16)
    p["b_glu"] = s5_b_glu[lyr].reshape(1, S5_CH)
    p["w_br_mla"] = w_br_mla[lyr].astype(BF16)
    p["w_br_gla"] = w_br_gla[lyr].astype(BF16)
    p["w_br_s5"] = w_br_s5[lyr].astype(BF16)
    p["w_out"] = w_out[lyr].astype(BF16)
    return p


def kernel(x, norm1_g, w_in, mla_q_norm_g, mla_w_uq, mla_kv_norm_g, mla_w_ukv, mla_q_head_g, mla_k_head_g,
           gla_w_gate, gla_b_gate, gla_out_g, s5_lam_re, s5_lam_im, s5_b_re, s5_b_im, s5_c_re, s5_c_im, s5_d,
           s5_log_dt, s5_w_glu, s5_b_glu, w_br_mla, w_br_gla, w_br_s5, gate_b, w_out, norm2_g, w_ff1, w_ff2):
    b, l, d = x.shape
    n = b * l
    depth = w_in.shape[0]
    tn = min(512, n)
    tl = min(512, l)
    tg = min(256, l)
    cos128, sin128 = _rope_tables(l)
    consts = _constants(tg)
    xf = x.reshape(n, d)
    for lyr in range(depth):
        p = _layer_params(lyr, w_in, gate_b, mla_q_norm_g, mla_w_uq, mla_kv_norm_g, mla_w_ukv, mla_q_head_g,
                          mla_k_head_g, gla_w_gate, gla_b_gate, gla_out_g, s5_w_glu, s5_b_glu, w_br_mla,
                          w_br_gla, w_br_s5, w_out)
        sp = _s5_params(s5_lam_re[lyr], s5_lam_im[lyr], s5_b_re[lyr], s5_b_im[lyr], s5_c_re[lyr], s5_c_im[lyr],
                        s5_d[lyr], s5_log_dt[lyr])
        zmla, zgla, su, gates = _in_proj(xf, norm1_g[lyr].reshape(1, d), p["w_in"], p["gate_b"], tn)
        q, k, v = _mla_prep(zmla.reshape(b, l, ZMLA_W), cos128, sin128, p, tl)
        oa = _flash(q, k, v, min(FLASH_TQ, l)).reshape(n, BRANCH_W)
        ob = _gla(zgla.reshape(b, l, ZGLA_W), p, consts, tg).reshape(n, BRANCH_W)
        y = _s5(su, sp, b, l)
        x1 = _merge(xf, oa, ob, y, gates, p, tn)
        xf = _ffn(x1, norm2_g[lyr].reshape(1, d), w_ff1[lyr].astype(BF16), w_ff2[lyr].astype(BF16), tn)
    return xf.reshape(b, l, d)
```

```python
import functools
import math

import numpy as np
import jax
import jax.numpy as jnp
from jax import lax
from jax.experimental import pallas as pl
from jax.experimental.pallas import tpu as pltpu

F32 = jnp.float32
BF16 = jnp.bfloat16
HIGHEST = lax.Precision.HIGHEST

D_MODEL = 1024
MLA_HEADS = 8
MLA_NOPE = 64
MLA_ROPE = 32
MLA_QK = MLA_NOPE + MLA_ROPE
MLA_V = 64
MLA_Q_RANK = 384
MLA_KV_RANK = 256
ROPE_BASE = 10000.0
GLA_HEADS = 4
GLA_DK = 64
GLA_DV = 128
GLA_GATE_RANK = 16
GLA_TAU = 16.0
GLA_CHUNK = 64
S5_CH = 512
S5_GROUP = 16
S5_GROUPS = S5_CH // S5_GROUP
S5_STATE = 64
N_BRANCH = 3
BRANCH_W = 512
D_FF = 4 * D_MODEL
EPS = 1e-6

LANES = 128
HEAD_SLOT = LANES
SUBLANES = 8
S5_T = 8
S5_GB = LANES // S5_GROUP
S5_NB = S5_GROUPS // S5_GB
S5_BW = S5_T * LANES
S5_SW = S5_GB * S5_STATE
VMEM_LIMIT = 56 * 1024 * 1024
NEG = -1e30
FLASH_TQ = 1024

ZMLA_W = MLA_Q_RANK + MLA_KV_RANK + 2 * LANES
ZGLA_W = 2 * GLA_HEADS * GLA_DK + 2 * GLA_HEADS * GLA_DV + LANES
GATE_W = N_BRANCH * D_MODEL


def _cparams(sem):
    return pltpu.CompilerParams(dimension_semantics=sem, vmem_limit_bytes=VMEM_LIMIT)


def _const_spec(shape):
    nd = len(shape)
    return pl.BlockSpec(shape, lambda *_: (0,) * nd)


def _dot(a, b):
    return jnp.dot(a, b, preferred_element_type=F32)


def _dot_nt(a, b):
    return lax.dot_general(a, b, (((1,), (1,)), ((), ())), preferred_element_type=F32)


def _in_proj_kernel(x_ref, g_ref, w_ref, gb_ref, zmla_ref, zgla_ref, su_ref, gate_ref, h_ref, s_ref, *, chunk):
    x = x_ref[...]
    ms = jnp.mean(x * x, axis=-1, keepdims=True)
    h_ref[...] = (x * lax.rsqrt(ms + EPS) * g_ref[...]).astype(BF16)
    col = 0
    for out_ref in (zmla_ref, zgla_ref):
        width = out_ref.shape[-1]
        for c0 in range(0, width, chunk):
            c1 = min(c0 + chunk, width)
            out_ref[:, c0:c1] = _dot(h_ref[...], w_ref[:, col + c0:col + c1]).astype(BF16)
        col += width
    su = _dot(h_ref[...], w_ref[:, col:col + S5_CH])
    col += S5_CH
    rows = s_ref.shape[1] // S5_T
    for b in range(S5_NB):
        s_ref[b] = su[:, b * LANES:(b + 1) * LANES]
        for t in range(S5_T):
            su_ref[b, :, t * LANES:(t + 1) * LANES] = s_ref[b, pl.ds(t, rows, stride=S5_T), :].astype(BF16)
    for c0 in range(0, GATE_W, chunk):
        pre = _dot(h_ref[...], w_ref[:, col + c0:col + c0 + chunk]) + gb_ref[:, c0:c0 + chunk]
        gate_ref[:, c0:c0 + chunk] = jax.nn.sigmoid(pre).astype(BF16)


def _in_proj(xf, g, w, gate_b, tn):
    n = xf.shape[0]
    wtot = w.shape[1]
    return pl.pallas_call(
        functools.partial(_in_proj_kernel, chunk=512),
        grid=(n // tn,),
        in_specs=[
            pl.BlockSpec((tn, D_MODEL), lambda i: (i, 0)),
            _const_spec((1, D_MODEL)),
            _const_spec((D_MODEL, wtot)),
            _const_spec((1, GATE_W)),
        ],
        out_specs=[
            pl.BlockSpec((tn, ZMLA_W), lambda i: (i, 0)),
            pl.BlockSpec((tn, ZGLA_W), lambda i: (i, 0)),
            pl.BlockSpec((S5_NB, tn // S5_T, S5_BW), lambda i: (0, i, 0)),
            pl.BlockSpec((tn, GATE_W), lambda i: (i, 0)),
        ],
        out_shape=[
            jax.ShapeDtypeStruct((n, ZMLA_W), BF16),
            jax.ShapeDtypeStruct((n, ZGLA_W), BF16),
            jax.ShapeDtypeStruct((S5_NB, n // S5_T, S5_BW), BF16),
            jax.ShapeDtypeStruct((n, GATE_W), BF16),
        ],
        scratch_shapes=[pltpu.VMEM((tn, D_MODEL), BF16), pltpu.VMEM((S5_NB, tn, LANES), F32)],
        compiler_params=_cparams(("arbitrary",)),
        name="in_proj",
    )(xf, g, w, gate_b)


def _mla_prep_kernel(z_ref, cos_ref, sin_ref, gqn_ref, gkvn_ref, wq_ref, wqs_ref, wk_ref, wv_ref,
                     gq_ref, gqs_ref, gk_ref, gks_ref, vone_ref, q_ref, k_ref, v_ref):
    cq = z_ref[0, :, 0:MLA_Q_RANK].astype(F32)
    ckv = z_ref[0, :, MLA_Q_RANK:MLA_Q_RANK + MLA_KV_RANK].astype(F32)
    o = MLA_Q_RANK + MLA_KV_RANK
    kpe = z_ref[0, :, o:o + LANES].astype(F32)
    kpe_sw = z_ref[0, :, o + LANES:o + 2 * LANES].astype(F32)
    cos = cos_ref[...]
    sin = sin_ref[...]

    cqn = (cq * lax.rsqrt(jnp.mean(cq * cq, axis=-1, keepdims=True) + EPS) * gqn_ref[...]).astype(BF16)
    ckvn = (ckv * lax.rsqrt(jnp.mean(ckv * ckv, axis=-1, keepdims=True) + EPS) * gkvn_ref[...]).astype(BF16)

    q_raw = _dot(cqn, wq_ref[...])
    q_sw = _dot(cqn, wqs_ref[...])
    k_nope = _dot(ckvn, wk_ref[...])
    v_all = _dot(ckvn, wv_ref[...]) + vone_ref[...]

    ssq_pe = jnp.sum(kpe * kpe, axis=-1, keepdims=True)
    q_scale = MLA_QK ** -0.5 * math.log2(math.e)

    cq_t = gq_ref[...] * cos
    sq_t = gqs_ref[...] * sin
    ck_t = gk_ref[...] * cos
    kpe_rot = kpe * ck_t + kpe_sw * (gks_ref[...] * sin)
    for h in range(MLA_HEADS):
        sl = slice(h * HEAD_SLOT, (h + 1) * HEAD_SLOT)
        qr, kn = q_raw[:, sl], k_nope[:, sl]
        rq = lax.rsqrt(jnp.sum(qr * qr, axis=-1, keepdims=True) * (1.0 / MLA_QK) + EPS) * q_scale
        qh = rq * (qr * cq_t + q_sw[:, sl] * sq_t)
        q_ref[0, h] = qh.astype(BF16)
        rk = lax.rsqrt((jnp.sum(kn * kn, axis=-1, keepdims=True) + ssq_pe) * (1.0 / MLA_QK) + EPS)
        kh = rk * (kn * ck_t + kpe_rot)
        k_ref[0, h] = kh.astype(BF16)
        v_ref[0, h] = v_all[:, sl].T.astype(BF16)


def _mla_prep(zmla, cos128, sin128, p, tl):
    b, l, _ = zmla.shape
    hw = MLA_HEADS * HEAD_SLOT
    head_out = jax.ShapeDtypeStruct((b, MLA_HEADS, l, HEAD_SLOT), BF16)
    head_spec = pl.BlockSpec((1, MLA_HEADS, tl, HEAD_SLOT), lambda i, j: (i, 0, j, 0))
    return pl.pallas_call(
        _mla_prep_kernel,
        grid=(b, l // tl),
        in_specs=[
            pl.BlockSpec((1, tl, ZMLA_W), lambda i, j: (i, j, 0)),
            pl.BlockSpec((tl, LANES), lambda i, j: (j, 0)),
            pl.BlockSpec((tl, LANES), lambda i, j: (j, 0)),
            _const_spec((1, MLA_Q_RANK)),
            _const_spec((1, MLA_KV_RANK)),
            _const_spec((MLA_Q_RANK, hw)),
            _const_spec((MLA_Q_RANK, hw)),
            _const_spec((MLA_KV_RANK, hw)),
            _const_spec((MLA_KV_RANK, hw)),
            _const_spec((1, LANES)),
            _const_spec((1, LANES)),
            _const_spec((1, LANES)),
            _const_spec((1, LANES)),
            _const_spec((1, hw)),
        ],
        out_specs=[head_spec, head_spec,
                   pl.BlockSpec((1, MLA_HEADS, HEAD_SLOT, tl), lambda i, j: (i, 0, 0, j))],
        out_shape=[head_out, head_out, jax.ShapeDtypeStruct((b, MLA_HEADS, HEAD_SLOT, l), BF16)],
        compiler_params=_cparams(("arbitrary", "arbitrary")),
        name="mla_prep",
    )(zmla, cos128, sin128, p["gqn"], p["gkvn"], p["wq"], p["wqs"], p["wk"], p["wv"],
      p["gq"], p["gqs"], p["gk"], p["gks"], p["vone"])


def _flash_kernel(q_ref, k_ref, vt_ref, o_ref, *, tq):
    qi = pl.program_id(2)
    krow = lax.broadcasted_iota(jnp.int32, (tq, tq), 0)
    qcol = lax.broadcasted_iota(jnp.int32, (tq, tq), 1)
    srow = lax.broadcasted_iota(jnp.int32, (HEAD_SLOT, tq), 0)

    def step(j, carry, masked):
        start = pl.multiple_of(j * tq, tq)
        scores = [_dot_nt(k_ref[0, hh, pl.ds(start, tq), :], q_ref[0, hh]) for hh in range(2)]
        new = []
        for hh in range(2):
            m, acc = carry[hh]
            s = scores[hh]
            if masked:
                s = jnp.where(krow <= qcol, s, NEG)
            m_new = jnp.maximum(m, jnp.max(s, axis=0, keepdims=True))
            alpha = jnp.exp2(m - m_new)
            p = jnp.exp2(s - m_new).astype(BF16)
            new.append((m_new, alpha * acc + _dot(vt_ref[0, hh, :, pl.ds(start, tq)], p)))
        return tuple(new)

    init = (jnp.full((1, tq), NEG, F32), jnp.zeros((HEAD_SLOT, tq), F32))
    carry = lax.fori_loop(0, qi, functools.partial(step, masked=False), (init, init))
    (_, acc0), (_, acc1) = step(qi, carry, True)
    o0 = acc0 / acc0[MLA_V:MLA_V + 1, :]
    o1 = acc1 / acc1[0:1, :]
    o_ref[0] = jnp.where(srow < MLA_V, o0, o1).T.astype(BF16)


def _flash(q, k, vt, tq):
    b, h, l, _ = q.shape
    return pl.pallas_call(
        functools.partial(_flash_kernel, tq=tq),
        grid=(b, h // 2, l // tq),
        in_specs=[pl.BlockSpec((1, 2, tq, HEAD_SLOT), lambda i, j, t: (i, j, t, 0)),
                  pl.BlockSpec((1, 2, l, HEAD_SLOT), lambda i, j, t: (i, j, 0, 0)),
                  pl.BlockSpec((1, 2, HEAD_SLOT, l), lambda i, j, t: (i, j, 0, 0))],
        out_specs=pl.BlockSpec((1, tq, LANES), lambda i, j, t: (i, t, j)),
        out_shape=jax.ShapeDtypeStruct((b, l, h * MLA_V), BF16),
        compiler_params=_cparams(("arbitrary", "arbitrary", "arbitrary")),
        name="flash",
    )(q, k, vt)


GLA_QW = GLA_HEADS * GLA_DK
GLA_VW = GLA_HEADS * GLA_DV


def _split_bf16(a):
    hi = a.astype(BF16)
    lo = (a - hi.astype(F32)).astype(BF16)
    return hi, lo


def _gla_kernel(z_ref, wg_ref, bg_ref, og_ref, tri_ref, upp_ref, mk_ref, mv_ref, mvt_ref, causal_ref, bdn_ref,
                o_ref, st_ref, oacc_ref, *, tg):
    @pl.when(pl.program_id(0) == 0)
    def _():
        st_ref[...] = jnp.zeros_like(st_ref)

    for bi in range(z_ref.shape[0]):
        _gla_block(z_ref.at[bi], wg_ref, bg_ref, og_ref, tri_ref, upp_ref, mk_ref, mv_ref, mvt_ref, causal_ref,
                   bdn_ref, o_ref.at[bi], st_ref.at[bi], oacc_ref.at[bi], tg)


def _gla_block(z_ref, wg_ref, bg_ref, og_ref, tri_ref, upp_ref, mk_ref, mv_ref, mvt_ref, causal_ref, bdn_ref,
               o_ref, st_ref, oacc_ref, tg):
    q = z_ref[:, 0:GLA_QW].astype(F32)
    k = z_ref[:, GLA_QW:2 * GLA_QW].astype(F32)
    v = z_ref[:, 2 * GLA_QW:2 * GLA_QW + GLA_VW]
    r = z_ref[:, 2 * GLA_QW + GLA_VW:2 * GLA_QW + 2 * GLA_VW].astype(F32)
    glr = z_ref[:, 2 * GLA_QW + 2 * GLA_VW:]

    pre = _dot(glr, wg_ref[...]) + bg_ref[...]
    la = (jnp.minimum(pre, 0.0) - jnp.log(1.0 + jnp.exp(-jnp.abs(pre)))) * (1.0 / GLA_TAU)
    la_hi, la_lo = _split_bf16(la)
    bc = _dot(tri_ref[...], la_hi) + _dot(tri_ref[...], la_lo)
    rem = _dot(upp_ref[...], la_hi) + _dot(upp_ref[...], la_lo)
    qt = (q * (GLA_DK ** -0.5) * jnp.exp(bc)).astype(BF16)
    kt = (k * jnp.exp(-bc)).astype(BF16)
    kend = (k * jnp.exp(rem)).astype(BF16)

    for c in range(tg // GLA_CHUNK):
        sl = slice(c * GLA_CHUNK, (c + 1) * GLA_CHUNK)
        qc, kc, kec, vc = qt[sl], kt[sl], kend[sl], v[sl]
        krows = jnp.concatenate([kc] * GLA_HEADS, axis=0) * mk_ref[...]
        a = _dot_nt(qc, krows)
        a = jnp.where(causal_ref[...] > 0, a, 0.0).astype(BF16)
        vbd = jnp.concatenate([vc] * GLA_HEADS, axis=0) * mv_ref[...]
        st = st_ref[...]
        o = _dot(a, vbd) + _dot_nt(qc, st.astype(BF16))
        oacc_ref[sl, :] = o
        dst = _dot(vc.astype(F32).T.astype(BF16), kec)
        dec = jnp.exp(bc[(c + 1) * GLA_CHUNK - 1:(c + 1) * GLA_CHUNK, :])
        st_ref[...] = st * dec + dst * mvt_ref[...]

    o = oacc_ref[...]
    ss = _dot((o * o).astype(BF16), bdn_ref[...])
    y = o * lax.rsqrt(ss * (1.0 / GLA_DV) + EPS) * og_ref[...]
    o_ref[...] = (y * (r * jax.nn.sigmoid(r))).astype(BF16)


def _gla(zgla, p, c, tg):
    b, l, _ = zgla.shape
    return pl.pallas_call(
        functools.partial(_gla_kernel, tg=tg),
        grid=(l // tg,),
        in_specs=[
            pl.BlockSpec((b, tg, ZGLA_W), lambda j: (0, j, 0)),
            _const_spec((LANES, GLA_QW)),
            _const_spec((1, GLA_QW)),
            _const_spec((1, GLA_VW)),
            _const_spec((tg, tg)),
            _const_spec((tg, tg)),
            _const_spec((GLA_HEADS * GLA_CHUNK, GLA_QW)),
            _const_spec((GLA_HEADS * GLA_CHUNK, GLA_VW)),
            _const_spec((GLA_VW, GLA_QW)),
            _const_spec((GLA_CHUNK, GLA_HEADS * GLA_CHUNK)),
            _const_spec((GLA_VW, GLA_VW)),
        ],
        out_specs=pl.BlockSpec((b, tg, GLA_VW), lambda j: (0, j, 0)),
        out_shape=jax.ShapeDtypeStruct((b, l, GLA_VW), BF16),
        scratch_shapes=[pltpu.VMEM((b, GLA_VW, GLA_QW), F32), pltpu.VMEM((b, tg, GLA_VW), F32)],
        compiler_params=_cparams(("arbitrary",)),
        name="gla",
    )(zgla, p["w_gate"], p["b_gate"], p["gla_og"], c["gla_tri"], c["gla_upp"], c["gla_mk"], c["gla_mv"],
      c["gla_mvt"], c["gla_causal"], c["gla_bdn"])


def _s5_state_kernel(x_ref, p_ref, dre_ref, dim_ref):
    d = _dot(x_ref[0], p_ref[0])
    dre_ref[...] = d[:, :S5_SW]
    dim_ref[...] = d[:, S5_SW:]


def _s5_state(xb, p8, tr):
    nb, r, _ = xb.shape
    out = jax.ShapeDtypeStruct((r, nb * S5_SW), F32)
    ospec = pl.BlockSpec((tr, S5_SW), lambda i, j: (j, i))
    return pl.pallas_call(
        _s5_state_kernel,
        grid=(nb, r // tr),
        in_specs=[
            pl.BlockSpec((1, tr, S5_BW), lambda i, j: (i, j, 0)),
            pl.BlockSpec((1, S5_BW, 2 * S5_SW), lambda i, j: (i, 0, 0)),
        ],
        out_specs=[ospec, ospec],
        out_shape=[out, out],
        compiler_params=_cparams(("arbitrary", "arbitrary")),
        name="s5_state",
    )(xb, p8)


def _s5_scan_kernel(dre_ref, dim_ref, are_ref, aim_ref, sre_ref, sim_ref, wre_ref, wim_ref, cre_ref, cim_ref,
                    *, cb, nbatch):
    @pl.when(pl.program_id(0) == 0)
    def _():
        wre_ref[...] = jnp.zeros_like(wre_ref)
        wim_ref[...] = jnp.zeros_like(wim_ref)
        cre_ref[...] = jnp.zeros_like(cre_ref)
        cim_ref[...] = jnp.zeros_like(cim_ref)

    nt = wre_ref.shape[0]
    for b in range(nbatch):
        for j in range(nt):
            wre_ref[j, pl.ds(b, cb, stride=SUBLANES), :] = dre_ref[b, :, j * LANES:(j + 1) * LANES]
            wim_ref[j, pl.ds(b, cb, stride=SUBLANES), :] = dim_ref[b, :, j * LANES:(j + 1) * LANES]
    are = jnp.broadcast_to(are_ref[...], cre_ref.shape)
    aim = jnp.broadcast_to(aim_ref[...], cre_ref.shape)

    def body(c, carry):
        s_re, s_im = carry
        rows = pl.ds(pl.multiple_of(c * SUBLANES, SUBLANES), SUBLANES)
        d_re = wre_ref[:, rows, :]
        d_im = wim_ref[:, rows, :]
        wre_ref[:, rows, :] = s_re
        wim_ref[:, rows, :] = s_im
        return are * s_re - aim * s_im + d_re, are * s_im + aim * s_re + d_im

    s_re, s_im = lax.fori_loop(0, cb, body, (cre_ref[...], cim_ref[...]))
    cre_ref[...] = s_re
    cim_ref[...] = s_im
    for b in range(nbatch):
        for j in range(nt):
            sre_ref[b, :, j * LANES:(j + 1) * LANES] = wre_ref[j, pl.ds(b, cb, stride=SUBLANES), :]
            sim_ref[b, :, j * LANES:(j + 1) * LANES] = wim_ref[j, pl.ds(b, cb, stride=SUBLANES), :]


def _s5_scan(dre, dim, are, aim, cb):
    b, c, sw = dre.shape
    nt = sw // LANES
    spec = pl.BlockSpec((b, cb, sw), lambda i: (0, i, 0))
    out = jax.ShapeDtypeStruct((b, c, sw), F32)
    work = pltpu.VMEM((nt, cb * SUBLANES, LANES), F32)
    carry = pltpu.VMEM((nt, SUBLANES, LANES), F32)
    return pl.pallas_call(
        functools.partial(_s5_scan_kernel, cb=cb, nbatch=b),
        grid=(c // cb,),
        in_specs=[spec, spec, _const_spec((nt, 1, LANES)), _const_spec((nt, 1, LANES))],
        out_specs=[spec, spec],
        out_shape=[out, out],
        scratch_shapes=[work, work, carry, carry],
        compiler_params=_cparams(("arbitrary",)),
        name="s5_scan",
    )(dre, dim, are.reshape(nt, 1, LANES), aim.reshape(nt, 1, LANES))


def _gelu_tanh(y):
    return 0.5 * y * (1.0 + jnp.tanh(math.sqrt(2.0 / math.pi) * (y + 0.044715 * (y * y * y))))


def _s5_out_kernel(x_ref, sre_ref, sim_ref, m_ref, q_ref, y_ref):
    s8 = jnp.concatenate([sre_ref[...], sim_ref[...]], axis=1).astype(BF16)
    y = _dot(x_ref[0], m_ref[0]) + _dot(s8, q_ref[0])
    y_ref[0] = _gelu_tanh(y).astype(BF16)


def _s5_out(xb, sre, sim, m8, q8, tr):
    nb, r, _ = xb.shape
    sspec = pl.BlockSpec((tr, S5_SW), lambda i, j: (j, i))
    xspec = pl.BlockSpec((1, tr, S5_BW), lambda i, j: (i, j, 0))
    return pl.pallas_call(
        _s5_out_kernel,
        grid=(nb, r // tr),
        in_specs=[
            xspec, sspec, sspec,
            pl.BlockSpec((1, S5_BW, S5_BW), lambda i, j: (i, 0, 0)),
            pl.BlockSpec((1, 2 * S5_SW, S5_BW), lambda i, j: (i, 0, 0)),
        ],
        out_specs=xspec,
        out_shape=jax.ShapeDtypeStruct((nb, r, S5_BW), BF16),
        compiler_params=_cparams(("arbitrary", "arbitrary")),
        name="s5_out",
    )(xb, sre, sim, m8, q8)


def _s5_params(lam_re, lam_im, b_re, b_im, c_re, c_im, d, log_dt):
    g = S5_GROUPS
    lre = jnp.minimum(lam_re.astype(F32), -1e-4)
    lim = lam_im.astype(F32)
    step = jnp.exp(log_dt.astype(F32))[:, None]
    pw = jnp.arange(S5_T + 1, dtype=F32)[:, None, None]
    mag = jnp.exp(pw * (lre * step)[None])
    ang = pw * (lim * step)[None]
    pre, pim = mag * jnp.cos(ang), mag * jnp.sin(ang)
    nr, ni = pre[1] - 1.0, pim[1]
    den = lre * lre + lim * lim
    cr = (nr * lre + ni * lim) / den
    ci = (ni * lre - nr * lim) / den
    bre = cr[..., None] * b_re - ci[..., None] * b_im
    bim = cr[..., None] * b_im + ci[..., None] * b_re
    lbr = pre[..., None] * bre[None] - pim[..., None] * bim[None]
    lbi = pre[..., None] * bim[None] + pim[..., None] * bre[None]
    kd = (jnp.einsum("gip,dgpj->dgij", c_re, lbr, precision=HIGHEST)
          - jnp.einsum("gip,dgpj->dgij", c_im, lbi, precision=HIGHEST))
    lag = np.arange(S5_T)[None, :] - np.arange(S5_T)[:, None]
    kst = kd[np.clip(lag, 0, S5_T)]
    kst = jnp.where((lag >= 0)[:, :, None, None, None], kst, 0.0)
    m = kst.transpose(2, 0, 4, 1, 3)
    eye_t = jnp.eye(S5_T, dtype=F32)[None, :, None, :, None]
    eye_i = jnp.eye(S5_GROUP, dtype=F32)[None, None, :, None, :]
    m = m + eye_t * eye_i * d.astype(F32)[:, None, None, None, :]
    rev = np.arange(S5_T - 1, -1, -1)
    p_re = lbr[rev].transpose(1, 0, 3, 2)
    p_im = lbi[rev].transpose(1, 0, 3, 2)
    qr = c_re[None] * pre[1:, :, None, :] - c_im[None] * pim[1:, :, None, :]
    qi = -(c_re[None] * pim[1:, :, None, :] + c_im[None] * pre[1:, :, None, :])
    q_re = qr.transpose(1, 3, 0, 2)
    q_im = qi.transpose(1, 3, 0, 2)

    nb, gb = S5_NB, S5_GB
    lanes_ti = np.arange(S5_BW) // LANES * S5_GROUP + np.arange(S5_BW) % S5_GROUP
    exp_ti = jnp.asarray(np.arange(LANES)[:, None] == lanes_ti[None, :], BF16)
    exp_n = jnp.asarray(np.arange(S5_STATE)[:, None] == (np.arange(S5_SW) % S5_STATE)[None, :], BF16)
    g_row_sgj = np.arange(S5_BW) // S5_GROUP % gb
    g_col_tgi = np.arange(S5_BW) % LANES // S5_GROUP
    g_gn = np.arange(S5_SW) // S5_STATE

    def expand(compact, expander, row_g, col_g):
        full = jnp.einsum("brk,kc->brc", compact.astype(BF16), expander, preferred_element_type=BF16)
        return full * jnp.asarray(row_g[:, None] == col_g[None, :], BF16)

    def rows_sgj(a):
        return a.reshape(nb, gb, S5_T, S5_GROUP, -1).transpose(0, 2, 1, 3, 4).reshape(nb, S5_BW, -1)

    m8 = expand(rows_sgj(m.reshape(g, S5_T, S5_GROUP, LANES)), exp_ti, g_row_sgj, g_col_tgi)
    p8 = jnp.concatenate([expand(rows_sgj(p), exp_n, g_row_sgj, g_gn) for p in (p_re, p_im)], axis=2)
    q8 = jnp.concatenate([expand(q.reshape(nb, S5_SW, LANES), exp_ti, g_gn, g_col_tgi) for q in (q_re, q_im)],
                         axis=1)
    are = pre[S5_T].reshape(1, g * S5_STATE)
    aim = pim[S5_T].reshape(1, g * S5_STATE)
    return m8, p8, q8, are, aim


def _s5(xb, sp, b, l):
    m8, p8, q8, are, aim = sp
    c = l // S5_T
    r = b * c
    tr = min(1024, r)
    dre, dim = _s5_state(xb, p8, tr)
    sw = dre.shape[-1]
    sre, sim = _s5_scan(dre.reshape(b, c, sw), dim.reshape(b, c, sw), are, aim, min(64, c))
    return _s5_out(xb, sre.reshape(r, sw), sim.reshape(r, sw), m8, q8, tr)


def _merge_kernel(x_ref, oa_ref, ob_ref, y_ref, g_ref, wa_ref, wb_ref, wc_ref, wglu_ref, bglu_ref, wout_ref,
                  o_ref, ys_ref):
    rows = y_ref.shape[1]
    for b in range(S5_NB):
        for t in range(S5_T):
            ys_ref[b, pl.ds(t, rows, stride=S5_T), :] = y_ref[b, :, t * LANES:(t + 1) * LANES].astype(F32)
    y = jnp.concatenate([ys_ref[b] for b in range(S5_NB)], axis=1)
    oc = (y * jax.nn.sigmoid(_dot(y.astype(BF16), wglu_ref[...]) + bglu_ref[...])).astype(BF16)
    merged = (g_ref[:, 0:D_MODEL].astype(F32) * _dot(oa_ref[...], wa_ref[...])
              + g_ref[:, D_MODEL:2 * D_MODEL].astype(F32) * _dot(ob_ref[...], wb_ref[...])
              + g_ref[:, 2 * D_MODEL:].astype(F32) * _dot(oc, wc_ref[...]))
    o_ref[...] = x_ref[...] + _dot(merged.astype(BF16), wout_ref[...])


def _merge(xf, oa, ob, y, gates, p, tn):
    n = xf.shape[0]
    row = lambda w: pl.BlockSpec((tn, w), lambda i: (i, 0))
    return pl.pallas_call(
        _merge_kernel,
        grid=(n // tn,),
        in_specs=[
            row(D_MODEL), row(BRANCH_W), row(BRANCH_W),
            pl.BlockSpec((S5_NB, tn // S5_T, S5_BW), lambda i: (0, i, 0)), row(GATE_W),
            _const_spec((BRANCH_W, D_MODEL)), _const_spec((BRANCH_W, D_MODEL)), _const_spec((BRANCH_W, D_MODEL)),
            _const_spec((S5_CH, S5_CH)), _const_spec((1, S5_CH)), _const_spec((D_MODEL, D_MODEL)),
        ],
        out_specs=row(D_MODEL),
        out_shape=jax.ShapeDtypeStruct((n, D_MODEL), F32),
        scratch_shapes=[pltpu.VMEM((S5_NB, tn, LANES), F32)],
        compiler_params=_cparams(("arbitrary",)),
        name="merge",
    )(xf, oa, ob, y, gates, p["w_br_mla"], p["w_br_gla"], p["w_br_s5"], p["w_glu"], p["b_glu"], p["w_out"])


def _ffn_kernel(x_ref, g_ref, w1_ref, w2_ref, o_ref, h_ref, *, chunk):
    x = x_ref[...]
    ms = jnp.mean(x * x, axis=-1, keepdims=True)
    h_ref[...] = (x * lax.rsqrt(ms + EPS) * g_ref[...]).astype(BF16)
    o_ref[...] = x
    for c0 in range(0, D_FF, chunk):
        a = jnp.maximum(_dot(h_ref[...], w1_ref[:, c0:c0 + chunk]), 0.0)
        o_ref[...] += _dot((a * a).astype(BF16), w2_ref[c0:c0 + chunk, :])


def _ffn(xf, g, w1, w2, tn):
    n = xf.shape[0]
    return pl.pallas_call(
        functools.partial(_ffn_kernel, chunk=512),
        grid=(n // tn,),
        in_specs=[
            pl.BlockSpec((tn, D_MODEL), lambda i: (i, 0)),
            _const_spec((1, D_MODEL)),
            _const_spec((D_MODEL, D_FF)),
            _const_spec((D_FF, D_MODEL)),
        ],
        out_specs=pl.BlockSpec((tn, D_MODEL), lambda i: (i, 0)),
        out_shape=jax.ShapeDtypeStruct((n, D_MODEL), F32),
        scratch_shapes=[pltpu.VMEM((tn, D_MODEL), BF16)],
        compiler_params=_cparams(("arbitrary",)),
        name="ffn",
    )(xf, g, w1, w2)


def _head_slots(w, width):
    k = w.shape[0]
    w = w.reshape(k, MLA_HEADS, width)
    return jnp.pad(w, ((0, 0), (0, 0), (0, HEAD_SLOT - width))).reshape(k, MLA_HEADS * HEAD_SLOT)


def _swap_rope_halves(a):
    half = MLA_ROPE // 2
    return jnp.concatenate([jnp.zeros_like(a[..., :MLA_NOPE]), a[..., MLA_NOPE + half:], a[..., MLA_NOPE:MLA_NOPE + half]],
                           axis=-1)


def _constants(tg):
    c = {}
    t = np.arange(tg)
    same = (t[:, None] // GLA_CHUNK) == (t[None, :] // GLA_CHUNK)
    c["gla_tri"] = jnp.asarray(same & (t[None, :] <= t[:, None]), BF16)
    c["gla_upp"] = jnp.asarray(same & (t[None, :] > t[:, None]), BF16)
    hs = np.arange(GLA_HEADS * GLA_CHUNK) // GLA_CHUNK
    hk = np.arange(GLA_QW) // GLA_DK
    hv = np.arange(GLA_VW) // GLA_DV
    c["gla_mk"] = jnp.asarray(hs[:, None] == hk[None, :], BF16)
    c["gla_mv"] = jnp.asarray(hs[:, None] == hv[None, :], BF16)
    c["gla_mvt"] = jnp.asarray(hv[:, None] == hk[None, :], F32)
    s_in = np.arange(GLA_HEADS * GLA_CHUNK) % GLA_CHUNK
    c["gla_causal"] = jnp.asarray(s_in[None, :] <= np.arange(GLA_CHUNK)[:, None], F32)
    c["gla_bdn"] = jnp.asarray(np.kron(np.eye(GLA_HEADS), np.ones((GLA_DV, GLA_DV))), BF16)
    return c


def _rope_tables(l):
    pos = jnp.arange(l, dtype=F32)
    inv_freq = ROPE_BASE ** (-jnp.arange(0, MLA_ROPE, 2, dtype=F32) / MLA_ROPE)
    ang = pos[:, None] * inv_freq[None, :]
    cos, sin = jnp.cos(ang), jnp.sin(ang)
    pad = jnp.zeros((l, LANES - MLA_QK), F32)
    cos128 = jnp.concatenate([jnp.ones((l, MLA_NOPE), F32), cos, cos, pad], axis=-1)
    sin128 = jnp.concatenate([jnp.zeros((l, MLA_NOPE), F32), -sin, sin, pad], axis=-1)
    return cos128, sin128


def _layer_params(lyr, w_in, gate_b, mla_q_norm_g, mla_w_uq, mla_kv_norm_g, mla_w_ukv, mla_q_head_g,
                  mla_k_head_g, gla_w_gate, gla_b_gate, gla_out_g, s5_w_glu, s5_b_glu, w_br_mla, w_br_gla,
                  w_br_s5, w_out):
    p = {}
    w = w_in[lyr]
    sizes = (MLA_Q_RANK, MLA_KV_RANK, MLA_ROPE, GLA_QW, GLA_QW, GLA_VW, GLA_GATE_RANK, GLA_VW, S5_CH, GATE_W)
    offs = np.concatenate([[0], np.cumsum(sizes)])
    cq, ckv, kpe, gq, gk, gv, glr, gr, su, gates = [w[:, offs[i]:offs[i + 1]] for i in range(len(sizes))]
    half = MLA_ROPE // 2
    z = lambda n: jnp.zeros((D_MODEL, n), w.dtype)
    kpe_sw = jnp.concatenate([kpe[:, half:], kpe[:, :half]], axis=1)
    p["w_in"] = jnp.concatenate(
        [cq, ckv, z(MLA_NOPE), kpe, z(LANES - MLA_QK), z(MLA_NOPE), kpe_sw, z(LANES - MLA_QK),
         gq, gk, gv, gr, glr, z(LANES - GLA_GATE_RANK), su, gates], axis=1).astype(BF16)
    p["gate_b"] = gate_b[lyr].reshape(1, GATE_W)

    p["gqn"] = mla_q_norm_g[lyr].reshape(1, MLA_Q_RANK)
    p["gkvn"] = mla_kv_norm_g[lyr].reshape(1, MLA_KV_RANK)
    wuq = mla_w_uq[lyr].reshape(MLA_Q_RANK, MLA_HEADS, MLA_QK)
    p["wq"] = _head_slots(wuq.reshape(MLA_Q_RANK, -1), MLA_QK).astype(BF16)
    p["wqs"] = _head_slots(_swap_rope_halves(wuq).reshape(MLA_Q_RANK, -1), MLA_QK).astype(BF16)
    wukv = mla_w_ukv[lyr].reshape(MLA_KV_RANK, MLA_HEADS, MLA_NOPE + MLA_V)
    p["wk"] = _head_slots(wukv[..., :MLA_NOPE].reshape(MLA_KV_RANK, -1), MLA_NOPE).astype(BF16)
    wv = wukv[..., MLA_NOPE:]
    zv = jnp.zeros_like(wv)
    even = (np.arange(MLA_HEADS) % 2 == 0)[None, :, None]
    p["wv"] = jnp.where(even, jnp.concatenate([wv, zv], -1), jnp.concatenate([zv, wv], -1)).reshape(
        MLA_KV_RANK, MLA_HEADS * HEAD_SLOT).astype(BF16)
    vone = np.zeros((MLA_HEADS, HEAD_SLOT), np.float32)
    vone[0::2, MLA_V] = 1.0
    vone[1::2, 0] = 1.0
    p["vone"] = jnp.asarray(vone.reshape(1, -1))
    pad = lambda g: jnp.pad(g, (0, LANES - MLA_QK)).reshape(1, LANES)
    p["gq"] = pad(mla_q_head_g[lyr])
    p["gqs"] = pad(_swap_rope_halves(mla_q_head_g[lyr]))
    p["gk"] = pad(mla_k_head_g[lyr])
    p["gks"] = pad(_swap_rope_halves(mla_k_head_g[lyr]))

    p["w_gate"] = jnp.pad(gla_w_gate[lyr], ((0, LANES - GLA_GATE_RANK), (0, 0))).astype(BF16)
    p["b_gate"] = gla_b_gate[lyr].reshape(1, GLA_QW)
    p["gla_og"] = jnp.tile(gla_out_g[lyr], GLA_HEADS).reshape(1, GLA_VW)

    p["w_glu"] = s5_w_glu[lyr].astype(BF16)
    p["b_glu"] = s5_b_glu[lyr].reshape(1, S5_CH)
    p["w_br_mla"] = w_br_mla[lyr].astype(BF16)
    p["w_br_gla"] = w_br_gla[lyr].astype(BF16)
    p["w_br_s5"] = w_br_s5[lyr].astype(BF16)
    p["w_out"] = w_out[lyr].astype(BF16)
    return p


def kernel(x, norm1_g, w_in, mla_q_norm_g, mla_w_uq, mla_kv_norm_g, mla_w_ukv, mla_q_head_g, mla_k_head_g,
           gla_w_gate, gla_b_gate, gla_out_g, s5_lam_re, s5_lam_im, s5_b_re, s5_b_im, s5_c_re, s5_c_im, s5_d,
           s5_log_dt, s5_w_glu, s5_b_glu, w_br_mla, w_br_gla, w_br_s5, gate_b, w_out, norm2_g, w_ff1, w_ff2):
    b, l, d = x.shape
    n = b * l
    depth = w_in.shape[0]
    tn = min(512, n)
    tl = min(512, l)
    tg = min(256, l)
    cos128, sin128 = _rope_tables(l)
    consts = _constants(tg)
    xf = x.reshape(n, d)
    for lyr in range(depth):
        p = _layer_params(lyr, w_in, gate_b, mla_q_norm_g, mla_w_uq, mla_kv_norm_g, mla_w_ukv, mla_q_head_g,
                          mla_k_head_g, gla_w_gate, gla_b_gate, gla_out_g, s5_w_glu, s5_b_glu, w_br_mla,
                          w_br_gla, w_br_s5, w_out)
        sp = _s5_params(s5_lam_re[lyr], s5_lam_im[lyr], s5_b_re[lyr], s5_b_im[lyr], s5_c_re[lyr], s5_c_im[lyr],
                        s5_d[lyr], s5_log_dt[lyr])
        zmla, zgla, su, gates = _in_proj(xf, norm1_g[lyr].reshape(1, d), p["w_in"], p["gate_b"], tn)
        q, k, v = _mla_prep(zmla.reshape(b, l, ZMLA_W), cos128, sin128, p, tl)
        oa = _flash(q, k, v, min(FLASH_TQ, l)).reshape(n, BRANCH_W)
        ob = _gla(zgla.reshape(b, l, ZGLA_W), p, consts, tg).reshape(n, BRANCH_W)
        y = _s5(su, sp, b, l)
        x1 = _merge(xf, oa, ob, y, gates, p, tn)
        xf = _ffn(x1, norm2_g[lyr].reshape(1, d), w_ff1[lyr].astype(BF16), w_ff2[lyr].astype(BF16), tn)
    return xf.reshape(b, l, d)
```

```python
import functools
import math

import numpy as np
import jax
import jax.numpy as jnp
from jax import lax
from jax.experimental import pallas as pl
from jax.experimental.pallas import tpu as pltpu

F32 = jnp.float32
BF16 = jnp.bfloat16
HIGHEST = lax.Precision.HIGHEST

D_MODEL = 1024
MLA_HEADS = 8
MLA_NOPE = 64
MLA_ROPE = 32
MLA_QK = MLA_NOPE + MLA_ROPE
MLA_V = 64
MLA_Q_RANK = 384
MLA_KV_RANK = 256
ROPE_BASE = 10000.0
GLA_HEADS = 4
GLA_DK = 64
GLA_DV = 128
GLA_GATE_RANK = 16
GLA_TAU = 16.0
GLA_CHUNK = 64
S5_CH = 512
S5_GROUP = 16
S5_GROUPS = S5_CH // S5_GROUP
S5_STATE = 64
N_BRANCH = 3
BRANCH_W = 512
D_FF = 4 * D_MODEL
EPS = 1e-6

LANES = 128
HEAD_SLOT = LANES
SUBLANES = 8
S5_T = 8
S5_GB = LANES // S5_GROUP
S5_NB = S5_GROUPS // S5_GB
S5_BW = S5_T * LANES
S5_SW = S5_GB * S5_STATE
VMEM_LIMIT = 56 * 1024 * 1024
NEG = -1e30
FLASH_TQ = 1024

ZMLA_W = MLA_Q_RANK + MLA_KV_RANK + 2 * LANES
ZGLA_W = 2 * GLA_HEADS * GLA_DK + 2 * GLA_HEADS * GLA_DV + LANES
GATE_W = N_BRANCH * D_MODEL


def _cparams(sem):
    return pltpu.CompilerParams(dimension_semantics=sem, vmem_limit_bytes=VMEM_LIMIT)


def _const_spec(shape):
    nd = len(shape)
    return pl.BlockSpec(shape, lambda *_: (0,) * nd)


def _dot(a, b):
    return jnp.dot(a, b, preferred_element_type=F32)


def _dot_nt(a, b):
    return lax.dot_general(a, b, (((1,), (1,)), ((), ())), preferred_element_type=F32)


def _in_proj_kernel(x_ref, g_ref, w_ref, gb_ref, zmla_ref, zgla_ref, su_ref, gate_ref, h_ref, s_ref, *, chunk):
    x = x_ref[...]
    ms = jnp.mean(x * x, axis=-1, keepdims=True)
    h_ref[...] = (x * lax.rsqrt(ms + EPS) * g_ref[...]).astype(BF16)
    col = 0
    for out_ref in (zmla_ref, zgla_ref):
        width = out_ref.shape[-1]
        for c0 in range(0, width, chunk):
            c1 = min(c0 + chunk, width)
            out_ref[:, c0:c1] = _dot(h_ref[...], w_ref[:, col + c0:col + c1]).astype(BF16)
        col += width
    su = _dot(h_ref[...], w_ref[:, col:col + S5_CH])
    col += S5_CH
    rows = s_ref.shape[1] // S5_T
    for b in range(S5_NB):
        s_ref[b] = su[:, b * LANES:(b + 1) * LANES]
        for t in range(S5_T):
            su_ref[b, :, t * LANES:(t + 1) * LANES] = s_ref[b, pl.ds(t, rows, stride=S5_T), :].astype(BF16)
    for c0 in range(0, GATE_W, chunk):
        pre = _dot(h_ref[...], w_ref[:, col + c0:col + c0 + chunk]) + gb_ref[:, c0:c0 + chunk]
        gate_ref[:, c0:c0 + chunk] = jax.nn.sigmoid(pre).astype(BF16)


def _in_proj(xf, g, w, gate_b, tn):
    n = xf.shape[0]
    wtot = w.shape[1]
    return pl.pallas_call(
        functools.partial(_in_proj_kernel, chunk=512),
        grid=(n // tn,),
        in_specs=[
            pl.BlockSpec((tn, D_MODEL), lambda i: (i, 0)),
            _const_spec((1, D_MODEL)),
            _const_spec((D_MODEL, wtot)),
            _const_spec((1, GATE_W)),
        ],
        out_specs=[
            pl.BlockSpec((tn, ZMLA_W), lambda i: (i, 0)),
            pl.BlockSpec((tn, ZGLA_W), lambda i: (i, 0)),
            pl.BlockSpec((S5_NB, tn // S5_T, S5_BW), lambda i: (0, i, 0)),
            pl.BlockSpec((tn, GATE_W), lambda i: (i, 0)),
        ],
        out_shape=[
            jax.ShapeDtypeStruct((n, ZMLA_W), BF16),
            jax.ShapeDtypeStruct((n, ZGLA_W), BF16),
            jax.ShapeDtypeStruct((S5_NB, n // S5_T, S5_BW), BF16),
            jax.ShapeDtypeStruct((n, GATE_W), BF16),
        ],
        scratch_shapes=[pltpu.VMEM((tn, D_MODEL), BF16), pltpu.VMEM((S5_NB, tn, LANES), F32)],
        compiler_params=_cparams(("arbitrary",)),
        name="in_proj",
    )(xf, g, w, gate_b)


def _mla_prep_kernel(z_ref, cos_ref, sin_ref, gqn_ref, gkvn_ref, wq_ref, wqs_ref, wk_ref, wv_ref,
                     gq_ref, gqs_ref, gk_ref, gks_ref, vone_ref, q_ref, k_ref, v_ref):
    cq = z_ref[0, :, 0:MLA_Q_RANK].astype(F32)
    ckv = z_ref[0, :, MLA_Q_RANK:MLA_Q_RANK + MLA_KV_RANK].astype(F32)
    o = MLA_Q_RANK + MLA_KV_RANK
    kpe = z_ref[0, :, o:o + LANES].astype(F32)
    kpe_sw = z_ref[0, :, o + LANES:o + 2 * LANES].astype(F32)
    cos = cos_ref[...]
    sin = sin_ref[...]

    cqn = (cq * lax.rsqrt(jnp.mean(cq * cq, axis=-1, keepdims=True) + EPS) * gqn_ref[...]).astype(BF16)
    ckvn = (ckv * lax.rsqrt(jnp.mean(ckv * ckv, axis=-1, keepdims=True) + EPS) * gkvn_ref[...]).astype(BF16)

    q_raw = _dot(cqn, wq_ref[...])
    q_sw = _dot(cqn, wqs_ref[...])
    k_nope = _dot(ckvn, wk_ref[...])
    v_all = _dot(ckvn, wv_ref[...]) + vone_ref[...]

    ssq_pe = jnp.sum(kpe * kpe, axis=-1, keepdims=True)
    q_scale = MLA_QK ** -0.5 * math.log2(math.e)

    cq_t = gq_ref[...] * cos
    sq_t = gqs_ref[...] * sin
    ck_t = gk_ref[...] * cos
    kpe_rot = kpe * ck_t + kpe_sw * (gks_ref[...] * sin)
    for h in range(MLA_HEADS):
        sl = slice(h * HEAD_SLOT, (h + 1) * HEAD_SLOT)
        qr, kn = q_raw[:, sl], k_nope[:, sl]
        rq = lax.rsqrt(jnp.sum(qr * qr, axis=-1, keepdims=True) * (1.0 / MLA_QK) + EPS) * q_scale
        qh = rq * (qr * cq_t + q_sw[:, sl] * sq_t)
        q_ref[0, h] = qh.astype(BF16)
        rk = lax.rsqrt((jnp.sum(kn * kn, axis=-1, keepdims=True) + ssq_pe) * (1.0 / MLA_QK) + EPS)
        kh = rk * (kn * ck_t + kpe_rot)
        k_ref[0, h] = kh.astype(BF16)
        v_ref[0, h] = v_all[:, sl].T.astype(BF16)


def _mla_prep(zmla, cos128, sin128, p, tl):
    b, l, _ = zmla.shape
    hw = MLA_HEADS * HEAD_SLOT
    head_out = jax.ShapeDtypeStruct((b, MLA_HEADS, l, HEAD_SLOT), BF16)
    head_spec = pl.BlockSpec((1, MLA_HEADS, tl, HEAD_SLOT), lambda i, j: (i, 0, j, 0))
    return pl.pallas_call(
        _mla_prep_kernel,
        grid=(b, l // tl),
        in_specs=[
            pl.BlockSpec((1, tl, ZMLA_W), lambda i, j: (i, j, 0)),
            pl.BlockSpec((tl, LANES), lambda i, j: (j, 0)),
            pl.BlockSpec((tl, LANES), lambda i, j: (j, 0)),
            _const_spec((1, MLA_Q_RANK)),
            _const_spec((1, MLA_KV_RANK)),
            _const_spec((MLA_Q_RANK, hw)),
            _const_spec((MLA_Q_RANK, hw)),
            _const_spec((MLA_KV_RANK, hw)),
            _const_spec((MLA_KV_RANK, hw)),
            _const_spec((1, LANES)),
            _const_spec((1, LANES)),
            _const_spec((1, LANES)),
            _const_spec((1, LANES)),
            _const_spec((1, hw)),
        ],
        out_specs=[head_spec, head_spec,
                   pl.BlockSpec((1, MLA_HEADS, HEAD_SLOT, tl), lambda i, j: (i, 0, 0, j))],
        out_shape=[head_out, head_out, jax.ShapeDtypeStruct((b, MLA_HEADS, HEAD_SLOT, l), BF16)],
        compiler_params=_cparams(("arbitrary", "arbitrary")),
        name="mla_prep",
    )(zmla, cos128, sin128, p["gqn"], p["gkvn"], p["wq"], p["wqs"], p["wk"], p["wv"],
      p["gq"], p["gqs"], p["gk"], p["gks"], p["vone"])


def _flash_kernel(q_ref, k_ref, vt_ref, o_ref, *, tq):
    qi = pl.program_id(2)
    srow = lax.broadcasted_iota(jnp.int32, (HEAD_SLOT, tq), 0)

    def _causal(nk, nq, q_off):
        return (lax.broadcasted_iota(jnp.int32, (nk, nq), 0)
                <= lax.broadcasted_iota(jnp.int32, (nk, nq), 1) + q_off)

    def update(m, acc, s, vt):
        sb = s.astype(BF16)
        m_new = jnp.maximum(m, jnp.max(sb, axis=0, keepdims=True).astype(F32))
        alpha = jnp.exp2(m - m_new)
        p = jnp.exp2(sb - m_new.astype(BF16))
        return m_new, alpha * acc + _dot(vt, p)

    half = tq // 2

    def step(j, carry):
        start = pl.multiple_of(j * tq, tq)
        scores = [_dot_nt(k_ref[0, hh, pl.ds(start, tq), :], q_ref[0, hh]) for hh in range(2)]
        new = []
        for hh in range(2):
            vt = vt_ref[0, hh, :, pl.ds(start, tq)]
            new.append((update(*carry[hh][0], scores[hh][:, :half], vt),
                        update(*carry[hh][1], scores[hh][:, half:], vt)))
        return tuple(new)

    def diagonal(carry):
        start = pl.multiple_of(qi * tq, tq)
        s_lo = [_dot_nt(k_ref[0, hh, pl.ds(start, half), :], q_ref[0, hh, 0:half, :]) for hh in range(2)]
        s_hi = [_dot_nt(k_ref[0, hh, pl.ds(start, tq), :], q_ref[0, hh, half:, :]) for hh in range(2)]
        accs = []
        for hh in range(2):
            lo = jnp.where(_causal(half, half, 0), s_lo[hh], NEG)
            hi = jnp.where(_causal(tq, half, half), s_hi[hh], NEG)
            _, a_lo = update(*carry[hh][0], lo, vt_ref[0, hh, :, pl.ds(start, half)])
            _, a_hi = update(*carry[hh][1], hi, vt_ref[0, hh, :, pl.ds(start, tq)])
            accs.append(jnp.concatenate([a_lo, a_hi], axis=1))
        return accs

    init = (jnp.full((1, half), NEG, F32), jnp.zeros((HEAD_SLOT, half), F32))
    carry = lax.fori_loop(0, qi, step, ((init, init), (init, init)))
    acc0, acc1 = diagonal(carry)
    o0 = acc0 / acc0[MLA_V:MLA_V + 1, :]
    o1 = acc1 / acc1[0:1, :]
    o_ref[0] = jnp.where(srow < MLA_V, o0, o1).T.astype(BF16)


def _flash(q, k, vt, tq):
    b, h, l, _ = q.shape
    return pl.pallas_call(
        functools.partial(_flash_kernel, tq=tq),
        grid=(b, h // 2, l // tq),
        in_specs=[pl.BlockSpec((1, 2, tq, HEAD_SLOT), lambda i, j, t: (i, j, t, 0)),
                  pl.BlockSpec((1, 2, l, HEAD_SLOT), lambda i, j, t: (i, j, 0, 0)),
                  pl.BlockSpec((1, 2, HEAD_SLOT, l), lambda i, j, t: (i, j, 0, 0))],
        out_specs=pl.BlockSpec((1, tq, LANES), lambda i, j, t: (i, t, j)),
        out_shape=jax.ShapeDtypeStruct((b, l, h * MLA_V), BF16),
        compiler_params=_cparams(("arbitrary", "arbitrary", "arbitrary")),
        name="flash",
    )(q, k, vt)


GLA_QW = GLA_HEADS * GLA_DK
GLA_VW = GLA_HEADS * GLA_DV


def _split_bf16(a):
    hi = a.astype(BF16)
    lo = (a - hi.astype(F32)).astype(BF16)
    return hi, lo


def _gla_kernel(z_ref, wg_ref, bg_ref, og_ref, tri_ref, upp_ref, mk_ref, mv_ref, mvt_ref, causal_ref, bdn_ref,
                o_ref, st_ref, oacc_ref, *, tg):
    @pl.when(pl.program_id(0) == 0)
    def _():
        st_ref[...] = jnp.zeros_like(st_ref)

    nb = z_ref.shape[0]
    nc = tg // GLA_CHUNK
    rows = [slice(c * GLA_CHUNK, (c + 1) * GLA_CHUNK) for c in range(nc)]

    pre = [_dot(z_ref[b, :, 2 * GLA_QW + 2 * GLA_VW:], wg_ref[...]) + bg_ref[...] for b in range(nb)]
    bc, qt, kt, kend = [], [], [], []
    for b in range(nb):
        la = (jnp.minimum(pre[b], 0.0) - jnp.log(1.0 + jnp.exp(-jnp.abs(pre[b])))) * (1.0 / GLA_TAU)
        la_hi, la_lo = _split_bf16(la)
        bc.append(_dot(tri_ref[...], la_hi) + _dot(tri_ref[...], la_lo))
        rem = _dot(upp_ref[...], la_hi) + _dot(upp_ref[...], la_lo)
        q = z_ref[b, :, 0:GLA_QW].astype(F32)
        k = z_ref[b, :, GLA_QW:2 * GLA_QW].astype(F32)
        qt.append((q * (GLA_DK ** -0.5) * jnp.exp(bc[b])).astype(BF16))
        kt.append((k * jnp.exp(-bc[b])).astype(BF16))
        kend.append((k * jnp.exp(rem)).astype(BF16))

    a, dst = {}, {}
    for b in range(nb):
        v = z_ref[b, :, 2 * GLA_QW:2 * GLA_QW + GLA_VW]
        for c in range(nc):
            krows = jnp.concatenate([kt[b][rows[c]]] * GLA_HEADS, axis=0) * mk_ref[...]
            a[b, c] = _dot_nt(qt[b][rows[c]], krows)
            dst[b, c] = _dot(v[rows[c]].astype(F32).T.astype(BF16), kend[b][rows[c]])
    for b in range(nb):
        v = z_ref[b, :, 2 * GLA_QW:2 * GLA_QW + GLA_VW]
        for c in range(nc):
            am = jnp.where(causal_ref[...] > 0, a[b, c], 0.0).astype(BF16)
            vbd = jnp.concatenate([v[rows[c]]] * GLA_HEADS, axis=0) * mv_ref[...]
            oacc_ref[b, rows[c], :] = _dot(am, vbd)
    for c in range(nc):
        for b in range(nb):
            st = st_ref[b]
            oacc_ref[b, rows[c], :] += _dot_nt(qt[b][rows[c]], st.astype(BF16))
            dec = jnp.exp(bc[b][(c + 1) * GLA_CHUNK - 1:(c + 1) * GLA_CHUNK, :])
            st_ref[b] = st * dec + dst[b, c] * mvt_ref[...]

    for b in range(nb):
        o = oacc_ref[b]
        r = z_ref[b, :, 2 * GLA_QW + GLA_VW:2 * GLA_QW + 2 * GLA_VW].astype(F32)
        ss = _dot((o * o).astype(BF16), bdn_ref[...])
        y = o * lax.rsqrt(ss * (1.0 / GLA_DV) + EPS) * og_ref[...]
        o_ref[b] = (y * (r * jax.nn.sigmoid(r))).astype(BF16)


def _gla(zgla, p, c, tg):
    b, l, _ = zgla.shape
    return pl.pallas_call(
        functools.partial(_gla_kernel, tg=tg),
        grid=(l // tg,),
        in_specs=[
            pl.BlockSpec((b, tg, ZGLA_W), lambda j: (0, j, 0)),
            _const_spec((LANES, GLA_QW)),
            _const_spec((1, GLA_QW)),
            _const_spec((1, GLA_VW)),
            _const_spec((tg, tg)),
            _const_spec((tg, tg)),
            _const_spec((GLA_HEADS * GLA_CHUNK, GLA_QW)),
            _const_spec((GLA_HEADS * GLA_CHUNK, GLA_VW)),
            _const_spec((GLA_VW, GLA_QW)),
            _const_spec((GLA_CHUNK, GLA_HEADS * GLA_CHUNK)),
            _const_spec((GLA_VW, GLA_VW)),
        ],
        out_specs=pl.BlockSpec((b, tg, GLA_VW), lambda j: (0, j, 0)),
        out_shape=jax.ShapeDtypeStruct((b, l, GLA_VW), BF16),
        scratch_shapes=[pltpu.VMEM((b, GLA_VW, GLA_QW), F32), pltpu.VMEM((b, tg, GLA_VW), F32)],
        compiler_params=_cparams(("arbitrary",)),
        name="gla",
    )(zgla, p["w_gate"], p["b_gate"], p["gla_og"], c["gla_tri"], c["gla_upp"], c["gla_mk"], c["gla_mv"],
      c["gla_mvt"], c["gla_causal"], c["gla_bdn"])


def _s5_state_kernel(x_ref, p_ref, dre_ref, dim_ref):
    d = _dot(x_ref[0], p_ref[0])
    dre_ref[...] = d[:, :S5_SW]
    dim_ref[...] = d[:, S5_SW:]


def _s5_state(xb, p8, tr):
    nb, r, _ = xb.shape
    out = jax.ShapeDtypeStruct((r, nb * S5_SW), F32)
    ospec = pl.BlockSpec((tr, S5_SW), lambda i, j: (j, i))
    return pl.pallas_call(
        _s5_state_kernel,
        grid=(nb, r // tr),
        in_specs=[
            pl.BlockSpec((1, tr, S5_BW), lambda i, j: (i, j, 0)),
            pl.BlockSpec((1, S5_BW, 2 * S5_SW), lambda i, j: (i, 0, 0)),
        ],
        out_specs=[ospec, ospec],
        out_shape=[out, out],
        compiler_params=_cparams(("arbitrary", "arbitrary")),
        name="s5_state",
    )(xb, p8)


def _s5_scan_kernel(dre_ref, dim_ref, are_ref, aim_ref, sre_ref, sim_ref, wre_ref, wim_ref, cre_ref, cim_ref,
                    *, cb, nbatch):
    @pl.when(pl.program_id(0) == 0)
    def _():
        wre_ref[...] = jnp.zeros_like(wre_ref)
        wim_ref[...] = jnp.zeros_like(wim_ref)
        cre_ref[...] = jnp.zeros_like(cre_ref)
        cim_ref[...] = jnp.zeros_like(cim_ref)

    nt = wre_ref.shape[0]
    for b in range(nbatch):
        for j in range(nt):
            wre_ref[j, pl.ds(b, cb, stride=SUBLANES), :] = dre_ref[b, :, j * LANES:(j + 1) * LANES]
            wim_ref[j, pl.ds(b, cb, stride=SUBLANES), :] = dim_ref[b, :, j * LANES:(j + 1) * LANES]
    are = jnp.broadcast_to(are_ref[...], cre_ref.shape)
    aim = jnp.broadcast_to(aim_ref[...], cre_ref.shape)

    def body(c, carry):
        s_re, s_im = carry
        rows = pl.ds(pl.multiple_of(c * SUBLANES, SUBLANES), SUBLANES)
        d_re = wre_ref[:, rows, :]
        d_im = wim_ref[:, rows, :]
        wre_ref[:, rows, :] = s_re
        wim_ref[:, rows, :] = s_im
        return are * s_re - aim * s_im + d_re, are * s_im + aim * s_re + d_im

    s_re, s_im = lax.fori_loop(0, cb, body, (cre_ref[...], cim_ref[...]))
    cre_ref[...] = s_re
    cim_ref[...] = s_im
    for b in range(nbatch):
        for j in range(nt):
            sre_ref[b, :, j * LANES:(j + 1) * LANES] = wre_ref[j, pl.ds(b, cb, stride=SUBLANES), :]
            sim_ref[b, :, j * LANES:(j + 1) * LANES] = wim_ref[j, pl.ds(b, cb, stride=SUBLANES), :]


def _s5_scan(dre, dim, are, aim, cb):
    b, c, sw = dre.shape
    nt = sw // LANES
    spec = pl.BlockSpec((b, cb, sw), lambda i: (0, i, 0))
    out = jax.ShapeDtypeStruct((b, c, sw), F32)
    work = pltpu.VMEM((nt, cb * SUBLANES, LANES), F32)
    carry = pltpu.VMEM((nt, SUBLANES, LANES), F32)
    return pl.pallas_call(
        functools.partial(_s5_scan_kernel, cb=cb, nbatch=b),
        grid=(c // cb,),
        in_specs=[spec, spec, _const_spec((nt, 1, LANES)), _const_spec((nt, 1, LANES))],
        out_specs=[spec, spec],
        out_shape=[out, out],
        scratch_shapes=[work, work, carry, carry],
        compiler_params=_cparams(("arbitrary",)),
        name="s5_scan",
    )(dre, dim, are.reshape(nt, 1, LANES), aim.reshape(nt, 1, LANES))


def _gelu_tanh(y):
    return 0.5 * y * (1.0 + jnp.tanh(math.sqrt(2.0 / math.pi) * (y + 0.044715 * (y * y * y))))


def _s5_out_kernel(x_ref, sre_ref, sim_ref, m_ref, q_ref, y_ref):
    s8 = jnp.concatenate([sre_ref[...], sim_ref[...]], axis=1).astype(BF16)
    y = _dot(x_ref[0], m_ref[0]) + _dot(s8, q_ref[0])
    y_ref[0] = _gelu_tanh(y).astype(BF16)


def _s5_out(xb, sre, sim, m8, q8, tr):
    nb, r, _ = xb.shape
    sspec = pl.BlockSpec((tr, S5_SW), lambda i, j: (j, i))
    xspec = pl.BlockSpec((1, tr, S5_BW), lambda i, j: (i, j, 0))
    return pl.pallas_call(
        _s5_out_kernel,
        grid=(nb, r // tr),
        in_specs=[
            xspec, sspec, sspec,
            pl.BlockSpec((1, S5_BW, S5_BW), lambda i, j: (i, 0, 0)),
            pl.BlockSpec((1, 2 * S5_SW, S5_BW), lambda i, j: (i, 0, 0)),
        ],
        out_specs=xspec,
        out_shape=jax.ShapeDtypeStruct((nb, r, S5_BW), BF16),
        compiler_params=_cparams(("arbitrary", "arbitrary")),
        name="s5_out",
    )(xb, sre, sim, m8, q8)


def _s5_params(lam_re, lam_im, b_re, b_im, c_re, c_im, d, log_dt):
    g = S5_GROUPS
    lre = jnp.minimum(lam_re.astype(F32), -1e-4)
    lim = lam_im.astype(F32)
    step = jnp.exp(log_dt.astype(F32))[:, None]
    pw = jnp.arange(S5_T + 1, dtype=F32)[:, None, None]
    mag = jnp.exp(pw * (lre * step)[None])
    ang = pw * (lim * step)[None]
    pre, pim = mag * jnp.cos(ang), mag * jnp.sin(ang)
    nr, ni = pre[1] - 1.0, pim[1]
    den = lre * lre + lim * lim
    cr = (nr * lre + ni * lim) / den
    ci = (ni * lre - nr * lim) / den
    bre = cr[..., None] * b_re - ci[..., None] * b_im
    bim = cr[..., None] * b_im + ci[..., None] * b_re
    lbr = pre[..., None] * bre[None] - pim[..., None] * bim[None]
    lbi = pre[..., None] * bim[None] + pim[..., None] * bre[None]
    kd = (jnp.einsum("gip,dgpj->dgij", c_re, lbr, precision=HIGHEST)
          - jnp.einsum("gip,dgpj->dgij", c_im, lbi, precision=HIGHEST))
    lag = np.arange(S5_T)[None, :] - np.arange(S5_T)[:, None]
    kst = kd[np.clip(lag, 0, S5_T)]
    kst = jnp.where((lag >= 0)[:, :, None, None, None], kst, 0.0)
    m = kst.transpose(2, 0, 4, 1, 3)
    eye_t = jnp.eye(S5_T, dtype=F32)[None, :, None, :, None]
    eye_i = jnp.eye(S5_GROUP, dtype=F32)[None, None, :, None, :]
    m = m + eye_t * eye_i * d.astype(F32)[:, None, None, None, :]
    rev = np.arange(S5_T - 1, -1, -1)
    p_re = lbr[rev].transpose(1, 0, 3, 2)
    p_im = lbi[rev].transpose(1, 0, 3, 2)
    qr = c_re[None] * pre[1:, :, None, :] - c_im[None] * pim[1:, :, None, :]
    qi = -(c_re[None] * pim[1:, :, None, :] + c_im[None] * pre[1:, :, None, :])
    q_re = qr.transpose(1, 3, 0, 2)
    q_im = qi.transpose(1, 3, 0, 2)

    nb, gb = S5_NB, S5_GB
    lanes_ti = np.arange(S5_BW) // LANES * S5_GROUP + np.arange(S5_BW) % S5_GROUP
    exp_ti = jnp.asarray(np.arange(LANES)[:, None] == lanes_ti[None, :], BF16)
    exp_n = jnp.asarray(np.arange(S5_STATE)[:, None] == (np.arange(S5_SW) % S5_STATE)[None, :], BF16)
    g_row_sgj = np.arange(S5_BW) // S5_GROUP % gb
    g_col_tgi = np.arange(S5_BW) % LANES // S5_GROUP
    g_gn = np.arange(S5_SW) // S5_STATE

    def expand(compact, expander, row_g, col_g):
        full = jnp.einsum("brk,kc->brc", compact.astype(BF16), expander, preferred_element_type=BF16)
        return full * jnp.asarray(row_g[:, None] == col_g[None, :], BF16)

    def rows_sgj(a):
        return a.reshape(nb, gb, S5_T, S5_GROUP, -1).transpose(0, 2, 1, 3, 4).reshape(nb, S5_BW, -1)

    m8 = expand(rows_sgj(m.reshape(g, S5_T, S5_GROUP, LANES)), exp_ti, g_row_sgj, g_col_tgi)
    p8 = jnp.concatenate([expand(rows_sgj(p), exp_n, g_row_sgj, g_gn) for p in (p_re, p_im)], axis=2)
    q8 = jnp.concatenate([expand(q.reshape(nb, S5_SW, LANES), exp_ti, g_gn, g_col_tgi) for q in (q_re, q_im)],
                         axis=1)
    are = pre[S5_T].reshape(1, g * S5_STATE)
    aim = pim[S5_T].reshape(1, g * S5_STATE)
    return m8, p8, q8, are, aim


def _s5(xb, sp, b, l):
    m8, p8, q8, are, aim = sp
    c = l // S5_T
    r = b * c
    tr = min(1024, r)
    dre, dim = _s5_state(xb, p8, tr)
    sw = dre.shape[-1]
    sre, sim = _s5_scan(dre.reshape(b, c, sw), dim.reshape(b, c, sw), are, aim, min(64, c))
    return _s5_out(xb, sre.reshape(r, sw), sim.reshape(r, sw), m8, q8, tr)


def _merge_kernel(x_ref, oa_ref, ob_ref, y_ref, g_ref, wa_ref, wb_ref, wc_ref, wglu_ref, bglu_ref, wout_ref,
                  o_ref, ys_ref):
    rows = y_ref.shape[1]
    for b in range(S5_NB):
        for t in range(S5_T):
            ys_ref[b, pl.ds(t, rows, stride=S5_T), :] = y_ref[b, :, t * LANES:(t + 1) * LANES].astype(F32)
    y = jnp.concatenate([ys_ref[b] for b in range(S5_NB)], axis=1)
    oc = (y * jax.nn.sigmoid(_dot(y.astype(BF16), wglu_ref[...]) + bglu_ref[...])).astype(BF16)
    merged = (g_ref[:, 0:D_MODEL].astype(F32) * _dot(oa_ref[...], wa_ref[...])
              + g_ref[:, D_MODEL:2 * D_MODEL].astype(F32) * _dot(ob_ref[...], wb_ref[...])
              + g_ref[:, 2 * D_MODEL:].astype(F32) * _dot(oc, wc_ref[...]))
    o_ref[...] = x_ref[...] + _dot(merged.astype(BF16), wout_ref[...])


def _merge(xf, oa, ob, y, gates, p, tn):
    n = xf.shape[0]
    row = lambda w: pl.BlockSpec((tn, w), lambda i: (i, 0))
    return pl.pallas_call(
        _merge_kernel,
        grid=(n // tn,),
        in_specs=[
            row(D_MODEL), row(BRANCH_W), row(BRANCH_W),
            pl.BlockSpec((S5_NB, tn // S5_T, S5_BW), lambda i: (0, i, 0)), row(GATE_W),
            _const_spec((BRANCH_W, D_MODEL)), _const_spec((BRANCH_W, D_MODEL)), _const_spec((BRANCH_W, D_MODEL)),
            _const_spec((S5_CH, S5_CH)), _const_spec((1, S5_CH)), _const_spec((D_MODEL, D_MODEL)),
        ],
        out_specs=row(D_MODEL),
        out_shape=jax.ShapeDtypeStruct((n, D_MODEL), F32),
        scratch_shapes=[pltpu.VMEM((S5_NB, tn, LANES), F32)],
        compiler_params=_cparams(("arbitrary",)),
        name="merge",
    )(xf, oa, ob, y, gates, p["w_br_mla"], p["w_br_gla"], p["w_br_s5"], p["w_glu"], p["b_glu"], p["w_out"])


def _ffn_kernel(x_ref, g_ref, w1_ref, w2_ref, o_ref, h_ref, *, chunk):
    x = x_ref[...]
    ms = jnp.mean(x * x, axis=-1, keepdims=True)
    h_ref[...] = (x * lax.rsqrt(ms + EPS) * g_ref[...]).astype(BF16)
    o_ref[...] = x
    for c0 in range(0, D_FF, chunk):
        a = jnp.maximum(_dot(h_ref[...], w1_ref[:, c0:c0 + chunk]), 0.0)
        o_ref[...] += _dot((a * a).astype(BF16), w2_ref[c0:c0 + chunk, :])


def _ffn(xf, g, w1, w2, tn):
    n = xf.shape[0]
    return pl.pallas_call(
        functools.partial(_ffn_kernel, chunk=512),
        grid=(n // tn,),
        in_specs=[
            pl.BlockSpec((tn, D_MODEL), lambda i: (i, 0)),
            _const_spec((1, D_MODEL)),
            _const_spec((D_MODEL, D_FF)),
            _const_spec((D_FF, D_MODEL)),
        ],
        out_specs=pl.BlockSpec((tn, D_MODEL), lambda i: (i, 0)),
        out_shape=jax.ShapeDtypeStruct((n, D_MODEL), F32),
        scratch_shapes=[pltpu.VMEM((tn, D_MODEL), BF16)],
        compiler_params=_cparams(("arbitrary",)),
        name="ffn",
    )(xf, g, w1, w2)


def _head_slots(w, width):
    k = w.shape[0]
    w = w.reshape(k, MLA_HEADS, width)
    return jnp.pad(w, ((0, 0), (0, 0), (0, HEAD_SLOT - width))).reshape(k, MLA_HEADS * HEAD_SLOT)


def _swap_rope_halves(a):
    half = MLA_ROPE // 2
    return jnp.concatenate([jnp.zeros_like(a[..., :MLA_NOPE]), a[..., MLA_NOPE + half:], a[..., MLA_NOPE:MLA_NOPE + half]],
                           axis=-1)


def _constants(tg):
    c = {}
    t = np.arange(tg)
    same = (t[:, None] // GLA_CHUNK) == (t[None, :] // GLA_CHUNK)
    c["gla_tri"] = jnp.asarray(same & (t[None, :] <= t[:, None]), BF16)
    c["gla_upp"] = jnp.asarray(same & (t[None, :] > t[:, None]), BF16)
    hs = np.arange(GLA_HEADS * GLA_CHUNK) // GLA_CHUNK
    hk = np.arange(GLA_QW) // GLA_DK
    hv = np.arange(GLA_VW) // GLA_DV
    c["gla_mk"] = jnp.asarray(hs[:, None] == hk[None, :], BF16)
    c["gla_mv"] = jnp.asarray(hs[:, None] == hv[None, :], BF16)
    c["gla_mvt"] = jnp.asarray(hv[:, None] == hk[None, :], F32)
    s_in = np.arange(GLA_HEADS * GLA_CHUNK) % GLA_CHUNK
    c["gla_causal"] = jnp.asarray(s_in[None, :] <= np.arange(GLA_CHUNK)[:, None], F32)
    c["gla_bdn"] = jnp.asarray(np.kron(np.eye(GLA_HEADS), np.ones((GLA_DV, GLA_DV))), BF16)
    return c


def _rope_tables(l):
    pos = jnp.arange(l, dtype=F32)
    inv_freq = ROPE_BASE ** (-jnp.arange(0, MLA_ROPE, 2, dtype=F32) / MLA_ROPE)
    ang = pos[:, None] * inv_freq[None, :]
    cos, sin = jnp.cos(ang), jnp.sin(ang)
    pad = jnp.zeros((l, LANES - MLA_QK), F32)
    cos128 = jnp.concatenate([jnp.ones((l, MLA_NOPE), F32), cos, cos, pad], axis=-1)
    sin128 = jnp.concatenate([jnp.zeros((l, MLA_NOPE), F32), -sin, sin, pad], axis=-1)
    return cos128, sin128


def _layer_params(lyr, w_in, gate_b, mla_q_norm_g, mla_w_uq, mla_kv_norm_g, mla_w_ukv, mla_q_head_g,
                  mla_k_head_g, gla_w_gate, gla_b_gate, gla_out_g, s5_w_glu, s5_b_glu, w_br_mla, w_br_gla,
                  w_br_s5, w_out):
    p = {}
    w = w_in[lyr]
    sizes = (MLA_Q_RANK, MLA_KV_RANK, MLA_ROPE, GLA_QW, GLA_QW, GLA_VW, GLA_GATE_RANK, GLA_VW, S5_CH, GATE_W)
    offs = np.concatenate([[0], np.cumsum(sizes)])
    cq, ckv, kpe, gq, gk, gv, glr, gr, su, gates = [w[:, offs[i]:offs[i + 1]] for i in range(len(sizes))]
    half = MLA_ROPE // 2
    z = lambda n: jnp.zeros((D_MODEL, n), w.dtype)
    kpe_sw = jnp.concatenate([kpe[:, half:], kpe[:, :half]], axis=1)
    p["w_in"] = jnp.concatenate(
        [cq, ckv, z(MLA_NOPE), kpe, z(LANES - MLA_QK), z(MLA_NOPE), kpe_sw, z(LANES - MLA_QK),
         gq, gk, gv, gr, glr, z(LANES - GLA_GATE_RANK), su, gates], axis=1).astype(BF16)
    p["gate_b"] = gate_b[lyr].reshape(1, GATE_W)

    p["gqn"] = mla_q_norm_g[lyr].reshape(1, MLA_Q_RANK)
    p["gkvn"] = mla_kv_norm_g[lyr].reshape(1, MLA_KV_RANK)
    wuq = mla_w_uq[lyr].reshape(MLA_Q_RANK, MLA_HEADS, MLA_QK)
    p["wq"] = _head_slots(wuq.reshape(MLA_Q_RANK, -1), MLA_QK).astype(BF16)
    p["wqs"] = _head_slots(_swap_rope_halves(wuq).reshape(MLA_Q_RANK, -1), MLA_QK).astype(BF16)
    wukv = mla_w_ukv[lyr].reshape(MLA_KV_RANK, MLA_HEADS, MLA_NOPE + MLA_V)
    p["wk"] = _head_slots(wukv[..., :MLA_NOPE].reshape(MLA_KV_RANK, -1), MLA_NOPE).astype(BF16)
    wv = wukv[..., MLA_NOPE:]
    zv = jnp.zeros_like(wv)
    even = (np.arange(MLA_HEADS) % 2 == 0)[None, :, None]
    p["wv"] = jnp.where(even, jnp.concatenate([wv, zv], -1), jnp.concatenate([zv, wv], -1)).reshape(
        MLA_KV_RANK, MLA_HEADS * HEAD_SLOT).astype(BF16)
    vone = np.zeros((MLA_HEADS, HEAD_SLOT), np.float32)
    vone[0::2, MLA_V] = 1.0
    vone[1::2, 0] = 1.0
    p["vone"] = jnp.asarray(vone.reshape(1, -1))
    pad = lambda g: jnp.pad(g, (0, LANES - MLA_QK)).reshape(1, LANES)
    p["gq"] = pad(mla_q_head_g[lyr])
    p["gqs"] = pad(_swap_rope_halves(mla_q_head_g[lyr]))
    p["gk"] = pad(mla_k_head_g[lyr])
    p["gks"] = pad(_swap_rope_halves(mla_k_head_g[lyr]))

    p["w_gate"] = jnp.pad(gla_w_gate[lyr], ((0, LANES - GLA_GATE_RANK), (0, 0))).astype(BF16)
    p["b_gate"] = gla_b_gate[lyr].reshape(1, GLA_QW)
    p["gla_og"] = jnp.tile(gla_out_g[lyr], GLA_HEADS).reshape(1, GLA_VW)

    p["w_glu"] = s5_w_glu[lyr].astype(BF16)
    p["b_glu"] = s5_b_glu[lyr].reshape(1, S5_CH)
    p["w_br_mla"] = w_br_mla[lyr].astype(BF16)
    p["w_br_gla"] = w_br_gla[lyr].astype(BF16)
    p["w_br_s5"] = w_br_s5[lyr].astype(BF16)
    p["w_out"] = w_out[lyr].astype(BF16)
    return p


def kernel(x, norm1_g, w_in, mla_q_norm_g, mla_w_uq, mla_kv_norm_g, mla_w_ukv, mla_q_head_g, mla_k_head_g,
           gla_w_gate, gla_b_gate, gla_out_g, s5_lam_re, s5_lam_im, s5_b_re, s5_b_im, s5_c_re, s5_c_im, s5_d,
           s5_log_dt, s5_w_glu, s5_b_glu, w_br_mla, w_br_gla, w_br_s5, gate_b, w_out, norm2_g, w_ff1, w_ff2):
    b, l, d = x.shape
    n = b * l
    depth = w_in.shape[0]
    tn = min(512, n)
    tl = min(512, l)
    tg = min(256, l)
    cos128, sin128 = _rope_tables(l)
    consts = _constants(tg)
    xf = x.reshape(n, d)
    for lyr in range(depth):
        p = _layer_params(lyr, w_in, gate_b, mla_q_norm_g, mla_w_uq, mla_kv_norm_g, mla_w_ukv, mla_q_head_g,
                          mla_k_head_g, gla_w_gate, gla_b_gate, gla_out_g, s5_w_glu, s5_b_glu, w_br_mla,
                          w_br_gla, w_br_s5, w_out)
        sp = _s5_params(s5_lam_re[lyr], s5_lam_im[lyr], s5_b_re[lyr], s5_b_im[lyr], s5_c_re[lyr], s5_c_im[lyr],
                        s5_d[lyr], s5_log_dt[lyr])
        zmla, zgla, su, gates = _in_proj(xf, norm1_g[lyr].reshape(1, d), p["w_in"], p["gate_b"], tn)
        q, k, v = _mla_prep(zmla.reshape(b, l, ZMLA_W), cos128, sin128, p, tl)
        oa = _flash(q, k, v, min(FLASH_TQ, l)).reshape(n, BRANCH_W)
        ob = _gla(zgla.reshape(b, l, ZGLA_W), p, consts, tg).reshape(n, BRANCH_W)
        y = _s5(su, sp, b, l)
        x1 = _merge(xf, oa, ob, y, gates, p, tn)
        xf = _ffn(x1, norm2_g[lyr].reshape(1, d), w_ff1[lyr].astype(BF16), w_ff2[lyr].astype(BF16), tn)
    return xf.reshape(b, l, d)
```

```python
import functools
import math

import numpy as np
import jax
import jax.numpy as jnp
from jax import lax
from jax.experimental import pallas as pl
from jax.experimental.pallas import tpu as pltpu

F32 = jnp.float32
BF16 = jnp.bfloat16
HIGHEST = lax.Precision.HIGHEST

D_MODEL = 1024
MLA_HEADS = 8
MLA_NOPE = 64
MLA_ROPE = 32
MLA_QK = MLA_NOPE + MLA_ROPE
MLA_V = 64
MLA_Q_RANK = 384
MLA_KV_RANK = 256
ROPE_BASE = 10000.0
GLA_HEADS = 4
GLA_DK = 64
GLA_DV = 128
GLA_GATE_RANK = 16
GLA_TAU = 16.0
GLA_CHUNK = 64
S5_CH = 512
S5_GROUP = 16
S5_GROUPS = S5_CH // S5_GROUP
S5_STATE = 64
N_BRANCH = 3
BRANCH_W = 512
D_FF = 4 * D_MODEL
EPS = 1e-6

LANES = 128
HEAD_SLOT = LANES
SUBLANES = 8
S5_T = 8
S5_GB = LANES // S5_GROUP
S5_NB = S5_GROUPS // S5_GB
S5_BW = S5_T * LANES
S5_SW = S5_GB * S5_STATE
VMEM_LIMIT = 56 * 1024 * 1024
NEG = -1e30
FLASH_TQ = 1024
FLASH_KEY_SLABS = 4

ZMLA_W = MLA_Q_RANK + MLA_KV_RANK + 2 * LANES
ZGLA_W = 2 * GLA_HEADS * GLA_DK + 2 * GLA_HEADS * GLA_DV + LANES
GATE_W = N_BRANCH * D_MODEL


def _cparams(sem):
    return pltpu.CompilerParams(dimension_semantics=sem, vmem_limit_bytes=VMEM_LIMIT)


def _const_spec(shape):
    nd = len(shape)
    return pl.BlockSpec(shape, lambda *_: (0,) * nd)


def _dot(a, b):
    return jnp.dot(a, b, preferred_element_type=F32)


def _dot_nt(a, b):
    return lax.dot_general(a, b, (((1,), (1,)), ((), ())), preferred_element_type=F32)


def _in_proj_kernel(x_ref, g_ref, w_ref, gb_ref, zmla_ref, zgla_ref, su_ref, gate_ref, h_ref, s_ref, *, chunk):
    x = x_ref[...]
    ms = jnp.mean(x * x, axis=-1, keepdims=True)
    h_ref[...] = (x * lax.rsqrt(ms + EPS) * g_ref[...]).astype(BF16)
    col = 0
    for out_ref in (zmla_ref, zgla_ref):
        width = out_ref.shape[-1]
        for c0 in range(0, width, chunk):
            c1 = min(c0 + chunk, width)
            out_ref[:, c0:c1] = _dot(h_ref[...], w_ref[:, col + c0:col + c1]).astype(BF16)
        col += width
    su = _dot(h_ref[...], w_ref[:, col:col + S5_CH])
    col += S5_CH
    rows = s_ref.shape[1] // S5_T
    for b in range(S5_NB):
        s_ref[b] = su[:, b * LANES:(b + 1) * LANES]
        for t in range(S5_T):
            su_ref[b, :, t * LANES:(t + 1) * LANES] = s_ref[b, pl.ds(t, rows, stride=S5_T), :].astype(BF16)
    for c0 in range(0, GATE_W, chunk):
        pre = _dot(h_ref[...], w_ref[:, col + c0:col + c0 + chunk]) + gb_ref[:, c0:c0 + chunk]
        gate_ref[:, c0:c0 + chunk] = jax.nn.sigmoid(pre).astype(BF16)


def _in_proj(xf, g, w, gate_b, tn):
    n = xf.shape[0]
    wtot = w.shape[1]
    return pl.pallas_call(
        functools.partial(_in_proj_kernel, chunk=512),
        grid=(n // tn,),
        in_specs=[
            pl.BlockSpec((tn, D_MODEL), lambda i: (i, 0)),
            _const_spec((1, D_MODEL)),
            _const_spec((D_MODEL, wtot)),
            _const_spec((1, GATE_W)),
        ],
        out_specs=[
            pl.BlockSpec((tn, ZMLA_W), lambda i: (i, 0)),
            pl.BlockSpec((tn, ZGLA_W), lambda i: (i, 0)),
            pl.BlockSpec((S5_NB, tn // S5_T, S5_BW), lambda i: (0, i, 0)),
            pl.BlockSpec((tn, GATE_W), lambda i: (i, 0)),
        ],
        out_shape=[
            jax.ShapeDtypeStruct((n, ZMLA_W), BF16),
            jax.ShapeDtypeStruct((n, ZGLA_W), BF16),
            jax.ShapeDtypeStruct((S5_NB, n // S5_T, S5_BW), BF16),
            jax.ShapeDtypeStruct((n, GATE_W), BF16),
        ],
        scratch_shapes=[pltpu.VMEM((tn, D_MODEL), BF16), pltpu.VMEM((S5_NB, tn, LANES), F32)],
        compiler_params=_cparams(("arbitrary",)),
        name="in_proj",
    )(xf, g, w, gate_b)


def _mla_prep_kernel(z_ref, cos_ref, sin_ref, gqn_ref, gkvn_ref, wq_ref, wqs_ref, wk_ref, wv_ref,
                     gq_ref, gqs_ref, gk_ref, gks_ref, vone_ref, q_ref, k_ref, v_ref):
    cq = z_ref[0, :, 0:MLA_Q_RANK].astype(F32)
    ckv = z_ref[0, :, MLA_Q_RANK:MLA_Q_RANK + MLA_KV_RANK].astype(F32)
    o = MLA_Q_RANK + MLA_KV_RANK
    kpe = z_ref[0, :, o:o + LANES].astype(F32)
    kpe_sw = z_ref[0, :, o + LANES:o + 2 * LANES].astype(F32)
    cos = cos_ref[...]
    sin = sin_ref[...]

    cqn = (cq * lax.rsqrt(jnp.mean(cq * cq, axis=-1, keepdims=True) + EPS) * gqn_ref[...]).astype(BF16)
    ckvn = (ckv * lax.rsqrt(jnp.mean(ckv * ckv, axis=-1, keepdims=True) + EPS) * gkvn_ref[...]).astype(BF16)

    q_raw = _dot(cqn, wq_ref[...])
    q_sw = _dot(cqn, wqs_ref[...])
    k_nope = _dot(ckvn, wk_ref[...])
    v_all = _dot(ckvn, wv_ref[...]) + vone_ref[...]

    ssq_pe = jnp.sum(kpe * kpe, axis=-1, keepdims=True)
    q_scale = MLA_QK ** -0.5 * math.log2(math.e)

    cq_t = gq_ref[...] * cos
    sq_t = gqs_ref[...] * sin
    ck_t = gk_ref[...] * cos
    kpe_rot = kpe * ck_t + kpe_sw * (gks_ref[...] * sin)
    for h in range(MLA_HEADS):
        sl = slice(h * HEAD_SLOT, (h + 1) * HEAD_SLOT)
        qr, kn = q_raw[:, sl], k_nope[:, sl]
        rq = lax.rsqrt(jnp.sum(qr * qr, axis=-1, keepdims=True) * (1.0 / MLA_QK) + EPS) * q_scale
        qh = rq * (qr * cq_t + q_sw[:, sl] * sq_t)
        q_ref[0, h] = qh.astype(BF16)
        rk = lax.rsqrt((jnp.sum(kn * kn, axis=-1, keepdims=True) + ssq_pe) * (1.0 / MLA_QK) + EPS)
        kh = rk * (kn * ck_t + kpe_rot)
        k_ref[0, h] = kh.astype(BF16)
        v_ref[0, h] = v_all[:, sl].T.astype(BF16)


def _mla_prep(zmla, cos128, sin128, p, tl):
    b, l, _ = zmla.shape
    hw = MLA_HEADS * HEAD_SLOT
    head_out = jax.ShapeDtypeStruct((b, MLA_HEADS, l, HEAD_SLOT), BF16)
    head_spec = pl.BlockSpec((1, MLA_HEADS, tl, HEAD_SLOT), lambda i, j: (i, 0, j, 0))
    return pl.pallas_call(
        _mla_prep_kernel,
        grid=(b, l // tl),
        in_specs=[
            pl.BlockSpec((1, tl, ZMLA_W), lambda i, j: (i, j, 0)),
            pl.BlockSpec((tl, LANES), lambda i, j: (j, 0)),
            pl.BlockSpec((tl, LANES), lambda i, j: (j, 0)),
            _const_spec((1, MLA_Q_RANK)),
            _const_spec((1, MLA_KV_RANK)),
            _const_spec((MLA_Q_RANK, hw)),
            _const_spec((MLA_Q_RANK, hw)),
            _const_spec((MLA_KV_RANK, hw)),
            _const_spec((MLA_KV_RANK, hw)),
            _const_spec((1, LANES)),
            _const_spec((1, LANES)),
            _const_spec((1, LANES)),
            _const_spec((1, LANES)),
            _const_spec((1, hw)),
        ],
        out_specs=[head_spec, head_spec,
                   pl.BlockSpec((1, MLA_HEADS, HEAD_SLOT, tl), lambda i, j: (i, 0, 0, j))],
        out_shape=[head_out, head_out, jax.ShapeDtypeStruct((b, MLA_HEADS, HEAD_SLOT, l), BF16)],
        compiler_params=_cparams(("arbitrary", "arbitrary")),
        name="mla_prep",
    )(zmla, cos128, sin128, p["gqn"], p["gkvn"], p["wq"], p["wqs"], p["wk"], p["wv"],
      p["gq"], p["gqs"], p["gk"], p["gks"], p["vone"])


def _flash_kernel(q_ref, k_ref, vt_ref, o_ref, *, tq):
    qi = pl.program_id(2)
    srow = lax.broadcasted_iota(jnp.int32, (HEAD_SLOT, tq), 0)

    def _causal(nk, nq, q_off):
        return (lax.broadcasted_iota(jnp.int32, (nk, nq), 0)
                <= lax.broadcasted_iota(jnp.int32, (nk, nq), 1) + q_off)

    def update(m, acc, s, vt):
        sb = s.astype(BF16)
        m_new = jnp.maximum(m, jnp.max(sb, axis=0, keepdims=True).astype(F32))
        alpha = jnp.exp2(m - m_new)
        p = jnp.exp2(sb - m_new.astype(BF16))
        return m_new, alpha * acc + _dot(vt, p)

    half = tq // 2

    kw = tq // FLASH_KEY_SLABS

    def step(j, carry):
        carry = [list(c) for c in carry]

        def slab_scores(hh, ks):
            start = pl.multiple_of(j * tq + ks * kw, kw)
            return _dot_nt(k_ref[0, hh, pl.ds(start, kw), :], q_ref[0, hh])

        pending = [slab_scores(hh, 0) for hh in range(2)]
        for ks in range(FLASH_KEY_SLABS):
            start = pl.multiple_of(j * tq + ks * kw, kw)
            for hh in range(2):
                s = pending[hh]
                if ks + 1 < FLASH_KEY_SLABS:
                    pending[hh] = slab_scores(hh, ks + 1)
                vt = vt_ref[0, hh, :, pl.ds(start, kw)]
                carry[hh] = [update(*carry[hh][0], s[:, :half], vt), update(*carry[hh][1], s[:, half:], vt)]
        return tuple(tuple(c) for c in carry)

    def diagonal(carry):
        start = pl.multiple_of(qi * tq, tq)
        s_lo = [_dot_nt(k_ref[0, hh, pl.ds(start, half), :], q_ref[0, hh, 0:half, :]) for hh in range(2)]
        s_hi = [_dot_nt(k_ref[0, hh, pl.ds(start, tq), :], q_ref[0, hh, half:, :]) for hh in range(2)]
        accs = []
        for hh in range(2):
            lo = jnp.where(_causal(half, half, 0), s_lo[hh], NEG)
            hi = jnp.where(_causal(tq, half, half), s_hi[hh], NEG)
            _, a_lo = update(*carry[hh][0], lo, vt_ref[0, hh, :, pl.ds(start, half)])
            _, a_hi = update(*carry[hh][1], hi, vt_ref[0, hh, :, pl.ds(start, tq)])
            accs.append(jnp.concatenate([a_lo, a_hi], axis=1))
        return accs

    init = (jnp.full((1, half), NEG, F32), jnp.zeros((HEAD_SLOT, half), F32))
    carry = lax.fori_loop(0, qi, step, ((init, init), (init, init)))
    acc0, acc1 = diagonal(carry)
    o0 = acc0 / acc0[MLA_V:MLA_V + 1, :]
    o1 = acc1 / acc1[0:1, :]
    o_ref[0] = jnp.where(srow < MLA_V, o0, o1).T.astype(BF16)


def _flash(q, k, vt, tq):
    b, h, l, _ = q.shape
    return pl.pallas_call(
        functools.partial(_flash_kernel, tq=tq),
        grid=(b, h // 2, l // tq),
        in_specs=[pl.BlockSpec((1, 2, tq, HEAD_SLOT), lambda i, j, t: (i, j, t, 0)),
                  pl.BlockSpec((1, 2, l, HEAD_SLOT), lambda i, j, t: (i, j, 0, 0)),
                  pl.BlockSpec((1, 2, HEAD_SLOT, l), lambda i, j, t: (i, j, 0, 0))],
        out_specs=pl.BlockSpec((1, tq, LANES), lambda i, j, t: (i, t, j)),
        out_shape=jax.ShapeDtypeStruct((b, l, h * MLA_V), BF16),
        compiler_params=_cparams(("arbitrary", "arbitrary", "arbitrary")),
        name="flash",
    )(q, k, vt)


GLA_QW = GLA_HEADS * GLA_DK
GLA_VW = GLA_HEADS * GLA_DV


def _split_bf16(a):
    hi = a.astype(BF16)
    lo = (a - hi.astype(F32)).astype(BF16)
    return hi, lo


def _gla_kernel(z_ref, wg_ref, bg_ref, og_ref, tri_ref, upp_ref, mk_ref, mv_ref, mvt_ref, causal_ref, bdn_ref,
                o_ref, st_ref, oacc_ref, *, tg):
    @pl.when(pl.program_id(0) == 0)
    def _():
        st_ref[...] = jnp.zeros_like(st_ref)

    nb = z_ref.shape[0]
    nc = tg // GLA_CHUNK
    rows = [slice(c * GLA_CHUNK, (c + 1) * GLA_CHUNK) for c in range(nc)]

    pre = [_dot(z_ref[b, :, 2 * GLA_QW + 2 * GLA_VW:], wg_ref[...]) + bg_ref[...] for b in range(nb)]
    bc, qt, kt, kend = [], [], [], []
    for b in range(nb):
        la = (jnp.minimum(pre[b], 0.0) - jnp.log(1.0 + jnp.exp(-jnp.abs(pre[b])))) * (1.0 / GLA_TAU)
        la_hi, la_lo = _split_bf16(la)
        bc.append(_dot(tri_ref[...], la_hi) + _dot(tri_ref[...], la_lo))
        rem = _dot(upp_ref[...], la_hi) + _dot(upp_ref[...], la_lo)
        q = z_ref[b, :, 0:GLA_QW].astype(F32)
        k = z_ref[b, :, GLA_QW:2 * GLA_QW].astype(F32)
        qt.append((q * (GLA_DK ** -0.5) * jnp.exp(bc[b])).astype(BF16))
        kt.append((k * jnp.exp(-bc[b])).astype(BF16))
        kend.append((k * jnp.exp(rem)).astype(BF16))

    v = [z_ref[b, :, 2 * GLA_QW:2 * GLA_QW + GLA_VW] for b in range(nb)]

    a = {}
    for b in range(nb):
        for c in range(nc):
            krows = jnp.concatenate([kt[b][rows[c]]] * GLA_HEADS, axis=0) * mk_ref[...]
            a[b, c] = _dot_nt(qt[b][rows[c]], krows)
    for b in range(nb):
        for c in range(nc):
            am = jnp.where(causal_ref[...] > 0, a[b, c], 0.0).astype(BF16)
            vbd = jnp.concatenate([v[b][rows[c]]] * GLA_HEADS, axis=0) * mv_ref[...]
            oacc_ref[b, rows[c], :] = _dot(am, vbd)

    def state_increments(c):
        return [_dot(v[b][rows[c]].astype(F32).T.astype(BF16), kend[b][rows[c]]) for b in range(nb)]

    dst = state_increments(0)
    for c in range(nc):
        dst_next = state_increments(c + 1) if c + 1 < nc else None
        for b in range(nb):
            st = st_ref[b]
            oacc_ref[b, rows[c], :] += _dot_nt(qt[b][rows[c]], st.astype(BF16) * mvt_ref[...])
            dec = jnp.exp(bc[b][(c + 1) * GLA_CHUNK - 1:(c + 1) * GLA_CHUNK, :])
            st_ref[b] = st * dec + dst[b]
        dst = dst_next

    for b in range(nb):
        o = oacc_ref[b]
        r = z_ref[b, :, 2 * GLA_QW + GLA_VW:2 * GLA_QW + 2 * GLA_VW].astype(F32)
        ss = _dot((o * o).astype(BF16), bdn_ref[...])
        y = o * lax.rsqrt(ss * (1.0 / GLA_DV) + EPS) * og_ref[...]
        o_ref[b] = (y * (r * jax.nn.sigmoid(r))).astype(BF16)


def _gla(zgla, p, c, tg):
    b, l, _ = zgla.shape
    return pl.pallas_call(
        functools.partial(_gla_kernel, tg=tg),
        grid=(l // tg,),
        in_specs=[
            pl.BlockSpec((b, tg, ZGLA_W), lambda j: (0, j, 0)),
            _const_spec((LANES, GLA_QW)),
            _const_spec((1, GLA_QW)),
            _const_spec((1, GLA_VW)),
            _const_spec((tg, tg)),
            _const_spec((tg, tg)),
            _const_spec((GLA_HEADS * GLA_CHUNK, GLA_QW)),
            _const_spec((GLA_HEADS * GLA_CHUNK, GLA_VW)),
            _const_spec((GLA_VW, GLA_QW)),
            _const_spec((GLA_CHUNK, GLA_HEADS * GLA_CHUNK)),
            _const_spec((GLA_VW, GLA_VW)),
        ],
        out_specs=pl.BlockSpec((b, tg, GLA_VW), lambda j: (0, j, 0)),
        out_shape=jax.ShapeDtypeStruct((b, l, GLA_VW), BF16),
        scratch_shapes=[pltpu.VMEM((b, GLA_VW, GLA_QW), F32), pltpu.VMEM((b, tg, GLA_VW), F32)],
        compiler_params=_cparams(("arbitrary",)),
        name="gla",
    )(zgla, p["w_gate"], p["b_gate"], p["gla_og"], c["gla_tri"], c["gla_upp"], c["gla_mk"], c["gla_mv"],
      c["gla_mvt"], c["gla_causal"], c["gla_bdn"])


def _s5_state_kernel(x_ref, p_ref, dre_ref, dim_ref):
    d = _dot(x_ref[0], p_ref[0])
    dre_ref[...] = d[:, :S5_SW]
    dim_ref[...] = d[:, S5_SW:]


def _s5_state(xb, p8, tr):
    nb, r, _ = xb.shape
    out = jax.ShapeDtypeStruct((r, nb * S5_SW), F32)
    ospec = pl.BlockSpec((tr, S5_SW), lambda i, j: (j, i))
    return pl.pallas_call(
        _s5_state_kernel,
        grid=(nb, r // tr),
        in_specs=[
            pl.BlockSpec((1, tr, S5_BW), lambda i, j: (i, j, 0)),
            pl.BlockSpec((1, S5_BW, 2 * S5_SW), lambda i, j: (i, 0, 0)),
        ],
        out_specs=[ospec, ospec],
        out_shape=[out, out],
        compiler_params=_cparams(("arbitrary", "arbitrary")),
        name="s5_state",
    )(xb, p8)


def _s5_scan_kernel(dre_ref, dim_ref, are_ref, aim_ref, sre_ref, sim_ref, wre_ref, wim_ref, cre_ref, cim_ref,
                    *, cb, nbatch):
    @pl.when(pl.program_id(0) == 0)
    def _():
        wre_ref[...] = jnp.zeros_like(wre_ref)
        wim_ref[...] = jnp.zeros_like(wim_ref)
        cre_ref[...] = jnp.zeros_like(cre_ref)
        cim_ref[...] = jnp.zeros_like(cim_ref)

    nt = wre_ref.shape[0]
    for b in range(nbatch):
        for j in range(nt):
            wre_ref[j, pl.ds(b, cb, stride=SUBLANES), :] = dre_ref[b, :, j * LANES:(j + 1) * LANES]
            wim_ref[j, pl.ds(b, cb, stride=SUBLANES), :] = dim_ref[b, :, j * LANES:(j + 1) * LANES]
    are = jnp.broadcast_to(are_ref[...], cre_ref.shape)
    aim = jnp.broadcast_to(aim_ref[...], cre_ref.shape)

    def body(c, carry):
        s_re, s_im = carry
        rows = pl.ds(pl.multiple_of(c * SUBLANES, SUBLANES), SUBLANES)
        d_re = wre_ref[:, rows, :]
        d_im = wim_ref[:, rows, :]
        wre_ref[:, rows, :] = s_re
        wim_ref[:, rows, :] = s_im
        return are * s_re - aim * s_im + d_re, are * s_im + aim * s_re + d_im

    s_re, s_im = lax.fori_loop(0, cb, body, (cre_ref[...], cim_ref[...]))
    cre_ref[...] = s_re
    cim_ref[...] = s_im
    for b in range(nbatch):
        for j in range(nt):
            sre_ref[b, :, j * LANES:(j + 1) * LANES] = wre_ref[j, pl.ds(b, cb, stride=SUBLANES), :]
            sim_ref[b, :, j * LANES:(j + 1) * LANES] = wim_ref[j, pl.ds(b, cb, stride=SUBLANES), :]


def _s5_scan(dre, dim, are, aim, cb):
    b, c, sw = dre.shape
    nt = sw // LANES
    spec = pl.BlockSpec((b, cb, sw), lambda i: (0, i, 0))
    out = jax.ShapeDtypeStruct((b, c, sw), F32)
    work = pltpu.VMEM((nt, cb * SUBLANES, LANES), F32)
    carry = pltpu.VMEM((nt, SUBLANES, LANES), F32)
    return pl.pallas_call(
        functools.partial(_s5_scan_kernel, cb=cb, nbatch=b),
        grid=(c // cb,),
        in_specs=[spec, spec, _const_spec((nt, 1, LANES)), _const_spec((nt, 1, LANES))],
        out_specs=[spec, spec],
        out_shape=[out, out],
        scratch_shapes=[work, work, carry, carry],
        compiler_params=_cparams(("arbitrary",)),
        name="s5_scan",
    )(dre, dim, are.reshape(nt, 1, LANES), aim.reshape(nt, 1, LANES))


def _gelu_tanh(y):
    return 0.5 * y * (1.0 + jnp.tanh(math.sqrt(2.0 / math.pi) * (y + 0.044715 * (y * y * y))))


def _s5_out_kernel(x_ref, sre_ref, sim_ref, m_ref, q_ref, y_ref):
    s8 = jnp.concatenate([sre_ref[...], sim_ref[...]], axis=1).astype(BF16)
    y = _dot(x_ref[0], m_ref[0]) + _dot(s8, q_ref[0])
    y_ref[0] = _gelu_tanh(y).astype(BF16)


def _s5_out(xb, sre, sim, m8, q8, tr):
    nb, r, _ = xb.shape
    sspec = pl.BlockSpec((tr, S5_SW), lambda i, j: (j, i))
    xspec = pl.BlockSpec((1, tr, S5_BW), lambda i, j: (i, j, 0))
    return pl.pallas_call(
        _s5_out_kernel,
        grid=(nb, r // tr),
        in_specs=[
            xspec, sspec, sspec,
            pl.BlockSpec((1, S5_BW, S5_BW), lambda i, j: (i, 0, 0)),
            pl.BlockSpec((1, 2 * S5_SW, S5_BW), lambda i, j: (i, 0, 0)),
        ],
        out_specs=xspec,
        out_shape=jax.ShapeDtypeStruct((nb, r, S5_BW), BF16),
        compiler_params=_cparams(("arbitrary", "arbitrary")),
        name="s5_out",
    )(xb, sre, sim, m8, q8)


def _s5_params(lam_re, lam_im, b_re, b_im, c_re, c_im, d, log_dt):
    g = S5_GROUPS
    lre = jnp.minimum(lam_re.astype(F32), -1e-4)
    lim = lam_im.astype(F32)
    step = jnp.exp(log_dt.astype(F32))[:, None]
    pw = jnp.arange(S5_T + 1, dtype=F32)[:, None, None]
    mag = jnp.exp(pw * (lre * step)[None])
    ang = pw * (lim * step)[None]
    pre, pim = mag * jnp.cos(ang), mag * jnp.sin(ang)
    nr, ni = pre[1] - 1.0, pim[1]
    den = lre * lre + lim * lim
    cr = (nr * lre + ni * lim) / den
    ci = (ni * lre - nr * lim) / den
    bre = cr[..., None] * b_re - ci[..., None] * b_im
    bim = cr[..., None] * b_im + ci[..., None] * b_re
    lbr = pre[..., None] * bre[None] - pim[..., None] * bim[None]
    lbi = pre[..., None] * bim[None] + pim[..., None] * bre[None]
    kd = (jnp.einsum("gip,dgpj->dgij", c_re, lbr, precision=HIGHEST)
          - jnp.einsum("gip,dgpj->dgij", c_im, lbi, precision=HIGHEST))
    lag = np.arange(S5_T)[None, :] - np.arange(S5_T)[:, None]
    kst = kd[np.clip(lag, 0, S5_T)]
    kst = jnp.where((lag >= 0)[:, :, None, None, None], kst, 0.0)
    m = kst.transpose(2, 0, 4, 1, 3)
    eye_t = jnp.eye(S5_T, dtype=F32)[None, :, None, :, None]
    eye_i = jnp.eye(S5_GROUP, dtype=F32)[None, None, :, None, :]
    m = m + eye_t * eye_i * d.astype(F32)[:, None, None, None, :]
    rev = np.arange(S5_T - 1, -1, -1)
    p_re = lbr[rev].transpose(1, 0, 3, 2)
    p_im = lbi[rev].transpose(1, 0, 3, 2)
    qr = c_re[None] * pre[1:, :, None, :] - c_im[None] * pim[1:, :, None, :]
    qi = -(c_re[None] * pim[1:, :, None, :] + c_im[None] * pre[1:, :, None, :])
    q_re = qr.transpose(1, 3, 0, 2)
    q_im = qi.transpose(1, 3, 0, 2)

    nb, gb = S5_NB, S5_GB
    lanes_ti = np.arange(S5_BW) // LANES * S5_GROUP + np.arange(S5_BW) % S5_GROUP
    exp_ti = jnp.asarray(np.arange(LANES)[:, None] == lanes_ti[None, :], BF16)
    exp_n = jnp.asarray(np.arange(S5_STATE)[:, None] == (np.arange(S5_SW) % S5_STATE)[None, :], BF16)
    g_row_sgj = np.arange(S5_BW) // S5_GROUP % gb
    g_col_tgi = np.arange(S5_BW) % LANES // S5_GROUP
    g_gn = np.arange(S5_SW) // S5_STATE

    def expand(compact, expander, row_g, col_g):
        full = jnp.einsum("brk,kc->brc", compact.astype(BF16), expander, preferred_element_type=BF16)
        return full * jnp.asarray(row_g[:, None] == col_g[None, :], BF16)

    def rows_sgj(a):
        return a.reshape(nb, gb, S5_T, S5_GROUP, -1).transpose(0, 2, 1, 3, 4).reshape(nb, S5_BW, -1)

    m8 = expand(rows_sgj(m.reshape(g, S5_T, S5_GROUP, LANES)), exp_ti, g_row_sgj, g_col_tgi)
    p8 = jnp.concatenate([expand(rows_sgj(p), exp_n, g_row_sgj, g_gn) for p in (p_re, p_im)], axis=2)
    q8 = jnp.concatenate([expand(q.reshape(nb, S5_SW, LANES), exp_ti, g_gn, g_col_tgi) for q in (q_re, q_im)],
                         axis=1)
    are = pre[S5_T].reshape(1, g * S5_STATE)
    aim = pim[S5_T].reshape(1, g * S5_STATE)
    return m8, p8, q8, are, aim


def _s5(xb, sp, b, l):
    m8, p8, q8, are, aim = sp
    c = l // S5_T
    r = b * c
    tr = min(1024, r)
    dre, dim = _s5_state(xb, p8, tr)
    sw = dre.shape[-1]
    sre, sim = _s5_scan(dre.reshape(b, c, sw), dim.reshape(b, c, sw), are, aim, min(64, c))
    return _s5_out(xb, sre.reshape(r, sw), sim.reshape(r, sw), m8, q8, tr)


def _merge_kernel(x_ref, oa_ref, ob_ref, y_ref, g_ref, wa_ref, wb_ref, wc_ref, wglu_ref, bglu_ref, wout_ref,
                  o_ref, ys_ref):
    rows = y_ref.shape[1]
    for b in range(S5_NB):
        for t in range(S5_T):
            ys_ref[b, pl.ds(t, rows, stride=S5_T), :] = y_ref[b, :, t * LANES:(t + 1) * LANES].astype(F32)
    y = jnp.concatenate([ys_ref[b] for b in range(S5_NB)], axis=1)
    oc = (y * jax.nn.sigmoid(_dot(y.astype(BF16), wglu_ref[...]) + bglu_ref[...])).astype(BF16)
    merged = (g_ref[:, 0:D_MODEL].astype(F32) * _dot(oa_ref[...], wa_ref[...])
              + g_ref[:, D_MODEL:2 * D_MODEL].astype(F32) * _dot(ob_ref[...], wb_ref[...])
              + g_ref[:, 2 * D_MODEL:].astype(F32) * _dot(oc, wc_ref[...]))
    o_ref[...] = x_ref[...] + _dot(merged.astype(BF16), wout_ref[...])


def _merge(xf, oa, ob, y, gates, p, tn):
    n = xf.shape[0]
    row = lambda w: pl.BlockSpec((tn, w), lambda i: (i, 0))
    return pl.pallas_call(
        _merge_kernel,
        grid=(n // tn,),
        in_specs=[
            row(D_MODEL), row(BRANCH_W), row(BRANCH_W),
            pl.BlockSpec((S5_NB, tn // S5_T, S5_BW), lambda i: (0, i, 0)), row(GATE_W),
            _const_spec((BRANCH_W, D_MODEL)), _const_spec((BRANCH_W, D_MODEL)), _const_spec((BRANCH_W, D_MODEL)),
            _const_spec((S5_CH, S5_CH)), _const_spec((1, S5_CH)), _const_spec((D_MODEL, D_MODEL)),
        ],
        out_specs=row(D_MODEL),
        out_shape=jax.ShapeDtypeStruct((n, D_MODEL), F32),
        scratch_shapes=[pltpu.VMEM((S5_NB, tn, LANES), F32)],
        compiler_params=_cparams(("arbitrary",)),
        name="merge",
    )(xf, oa, ob, y, gates, p["w_br_mla"], p["w_br_gla"], p["w_br_s5"], p["w_glu"], p["b_glu"], p["w_out"])


def _ffn_kernel(x_ref, g_ref, w1_ref, w2_ref, o_ref, h_ref, *, chunk):
    x = x_ref[...]
    ms = jnp.mean(x * x, axis=-1, keepdims=True)
    h_ref[...] = (x * lax.rsqrt(ms + EPS) * g_ref[...]).astype(BF16)
    o_ref[...] = x
    for c0 in range(0, D_FF, chunk):
        a = jnp.maximum(_dot(h_ref[...], w1_ref[:, c0:c0 + chunk]), 0.0)
        o_ref[...] += _dot((a * a).astype(BF16), w2_ref[c0:c0 + chunk, :])


def _ffn(xf, g, w1, w2, tn):
    n = xf.shape[0]
    return pl.pallas_call(
        functools.partial(_ffn_kernel, chunk=512),
        grid=(n // tn,),
        in_specs=[
            pl.BlockSpec((tn, D_MODEL), lambda i: (i, 0)),
            _const_spec((1, D_MODEL)),
            _const_spec((D_MODEL, D_FF)),
            _const_spec((D_FF, D_MODEL)),
        ],
        out_specs=pl.BlockSpec((tn, D_MODEL), lambda i: (i, 0)),
        out_shape=jax.ShapeDtypeStruct((n, D_MODEL), F32),
        scratch_shapes=[pltpu.VMEM((tn, D_MODEL), BF16)],
        compiler_params=_cparams(("arbitrary",)),
        name="ffn",
    )(xf, g, w1, w2)


def _head_slots(w, width):
    k = w.shape[0]
    w = w.reshape(k, MLA_HEADS, width)
    return jnp.pad(w, ((0, 0), (0, 0), (0, HEAD_SLOT - width))).reshape(k, MLA_HEADS * HEAD_SLOT)


def _swap_rope_halves(a):
    half = MLA_ROPE // 2
    return jnp.concatenate([jnp.zeros_like(a[..., :MLA_NOPE]), a[..., MLA_NOPE + half:], a[..., MLA_NOPE:MLA_NOPE + half]],
                           axis=-1)


def _constants(tg):
    c = {}
    t = np.arange(tg)
    same = (t[:, None] // GLA_CHUNK) == (t[None, :] // GLA_CHUNK)
    c["gla_tri"] = jnp.asarray(same & (t[None, :] <= t[:, None]), BF16)
    c["gla_upp"] = jnp.asarray(same & (t[None, :] > t[:, None]), BF16)
    hs = np.arange(GLA_HEADS * GLA_CHUNK) // GLA_CHUNK
    hk = np.arange(GLA_QW) // GLA_DK
    hv = np.arange(GLA_VW) // GLA_DV
    c["gla_mk"] = jnp.asarray(hs[:, None] == hk[None, :], BF16)
    c["gla_mv"] = jnp.asarray(hs[:, None] == hv[None, :], BF16)
    c["gla_mvt"] = jnp.asarray(hv[:, None] == hk[None, :], BF16)
    s_in = np.arange(GLA_HEADS * GLA_CHUNK) % GLA_CHUNK
    c["gla_causal"] = jnp.asarray(s_in[None, :] <= np.arange(GLA_CHUNK)[:, None], F32)
    c["gla_bdn"] = jnp.asarray(np.kron(np.eye(GLA_HEADS), np.ones((GLA_DV, GLA_DV))), BF16)
    return c


def _rope_tables(l):
    pos = jnp.arange(l, dtype=F32)
    inv_freq = ROPE_BASE ** (-jnp.arange(0, MLA_ROPE, 2, dtype=F32) / MLA_ROPE)
    ang = pos[:, None] * inv_freq[None, :]
    cos, sin = jnp.cos(ang), jnp.sin(ang)
    pad = jnp.zeros((l, LANES - MLA_QK), F32)
    cos128 = jnp.concatenate([jnp.ones((l, MLA_NOPE), F32), cos, cos, pad], axis=-1)
    sin128 = jnp.concatenate([jnp.zeros((l, MLA_NOPE), F32), -sin, sin, pad], axis=-1)
    return cos128, sin128


def _layer_params(lyr, w_in, gate_b, mla_q_norm_g, mla_w_uq, mla_kv_norm_g, mla_w_ukv, mla_q_head_g,
                  mla_k_head_g, gla_w_gate, gla_b_gate, gla_out_g, s5_w_glu, s5_b_glu, w_br_mla, w_br_gla,
                  w_br_s5, w_out):
    p = {}
    w = w_in[lyr]
    sizes = (MLA_Q_RANK, MLA_KV_RANK, MLA_ROPE, GLA_QW, GLA_QW, GLA_VW, GLA_GATE_RANK, GLA_VW, S5_CH, GATE_W)
    offs = np.concatenate([[0], np.cumsum(sizes)])
    cq, ckv, kpe, gq, gk, gv, glr, gr, su, gates = [w[:, offs[i]:offs[i + 1]] for i in range(len(sizes))]
    half = MLA_ROPE // 2
    z = lambda n: jnp.zeros((D_MODEL, n), w.dtype)
    kpe_sw = jnp.concatenate([kpe[:, half:], kpe[:, :half]], axis=1)
    p["w_in"] = jnp.concatenate(
        [cq, ckv, z(MLA_NOPE), kpe, z(LANES - MLA_QK), z(MLA_NOPE), kpe_sw, z(LANES - MLA_QK),
         gq, gk, gv, gr, glr, z(LANES - GLA_GATE_RANK), su, gates], axis=1).astype(BF16)
    p["gate_b"] = gate_b[lyr].reshape(1, GATE_W)

    p["gqn"] = mla_q_norm_g[lyr].reshape(1, MLA_Q_RANK)
    p["gkvn"] = mla_kv_norm_g[lyr].reshape(1, MLA_KV_RANK)
    wuq = mla_w_uq[lyr].reshape(MLA_Q_RANK, MLA_HEADS, MLA_QK)
    p["wq"] = _head_slots(wuq.reshape(MLA_Q_RANK, -1), MLA_QK).astype(BF16)
    p["wqs"] = _head_slots(_swap_rope_halves(wuq).reshape(MLA_Q_RANK, -1), MLA_QK).astype(BF16)
    wukv = mla_w_ukv[lyr].reshape(MLA_KV_RANK, MLA_HEADS, MLA_NOPE + MLA_V)
    p["wk"] = _head_slots(wukv[..., :MLA_NOPE].reshape(MLA_KV_RANK, -1), MLA_NOPE).astype(BF16)
    wv = wukv[..., MLA_NOPE:]
    zv = jnp.zeros_like(wv)
    even = (np.arange(MLA_HEADS) % 2 == 0)[None, :, None]
    p["wv"] = jnp.where(even, jnp.concatenate([wv, zv], -1), jnp.concatenate([zv, wv], -1)).reshape(
        MLA_KV_RANK, MLA_HEADS * HEAD_SLOT).astype(BF16)
    vone = np.zeros((MLA_HEADS, HEAD_SLOT), np.float32)
    vone[0::2, MLA_V] = 1.0
    vone[1::2, 0] = 1.0
    p["vone"] = jnp.asarray(vone.reshape(1, -1))
    pad = lambda g: jnp.pad(g, (0, LANES - MLA_QK)).reshape(1, LANES)
    p["gq"] = pad(mla_q_head_g[lyr])
    p["gqs"] = pad(_swap_rope_halves(mla_q_head_g[lyr]))
    p["gk"] = pad(mla_k_head_g[lyr])
    p["gks"] = pad(_swap_rope_halves(mla_k_head_g[lyr]))

    p["w_gate"] = jnp.pad(gla_w_gate[lyr], ((0, LANES - GLA_GATE_RANK), (0, 0))).astype(BF16)
    p["b_gate"] = gla_b_gate[lyr].reshape(1, GLA_QW)
    p["gla_og"] = jnp.tile(gla_out_g[lyr], GLA_HEADS).reshape(1, GLA_VW)

    p["w_glu"] = s5_w_glu[lyr].astype(BF16)
    p["b_glu"] = s5_b_glu[lyr].reshape(1, S5_CH)
    p["w_br_mla"] = w_br_mla[lyr].astype(BF16)
    p["w_br_gla"] = w_br_gla[lyr].astype(BF16)
    p["w_br_s5"] = w_br_s5[lyr].astype(BF16)
    p["w_out"] = w_out[lyr].astype(BF16)
    return p


def kernel(x, norm1_g, w_in, mla_q_norm_g, mla_w_uq, mla_kv_norm_g, mla_w_ukv, mla_q_head_g, mla_k_head_g,
           gla_w_gate, gla_b_gate, gla_out_g, s5_lam_re, s5_lam_im, s5_b_re, s5_b_im, s5_c_re, s5_c_im, s5_d,
           s5_log_dt, s5_w_glu, s5_b_glu, w_br_mla, w_br_gla, w_br_s5, gate_b, w_out, norm2_g, w_ff1, w_ff2):
    b, l, d = x.shape
    n = b * l
    depth = w_in.shape[0]
    tn = min(512, n)
    tl = min(512, l)
    tg = min(256, l)
    cos128, sin128 = _rope_tables(l)
    consts = _constants(tg)
    xf = x.reshape(n, d)
    for lyr in range(depth):
        p = _layer_params(lyr, w_in, gate_b, mla_q_norm_g, mla_w_uq, mla_kv_norm_g, mla_w_ukv, mla_q_head_g,
                          mla_k_head_g, gla_w_gate, gla_b_gate, gla_out_g, s5_w_glu, s5_b_glu, w_br_mla,
                          w_br_gla, w_br_s5, w_out)
        sp = _s5_params(s5_lam_re[lyr], s5_lam_im[lyr], s5_b_re[lyr], s5_b_im[lyr], s5_c_re[lyr], s5_c_im[lyr],
                        s5_d[lyr], s5_log_dt[lyr])
        zmla, zgla, su, gates = _in_proj(xf, norm1_g[lyr].reshape(1, d), p["w_in"], p["gate_b"], tn)
        q, k, v = _mla_prep(zmla.reshape(b, l, ZMLA_W), cos128, sin128, p, tl)
        oa = _flash(q, k, v, min(FLASH_TQ, l)).reshape(n, BRANCH_W)
        ob = _gla(zgla.reshape(b, l, ZGLA_W), p, consts, tg).reshape(n, BRANCH_W)
        y = _s5(su, sp, b, l)
        x1 = _merge(xf, oa, ob, y, gates, p, tn)
        xf = _ffn(x1, norm2_g[lyr].reshape(1, d), w_ff1[lyr].astype(BF16), w_ff2[lyr].astype(BF16), tn)
    return xf.reshape(b, l, d)
```

```python
import functools
import math

import numpy as np
import jax
import jax.numpy as jnp
from jax import lax
from jax.experimental import pallas as pl
from jax.experimental.pallas import tpu as pltpu

F32 = jnp.float32
BF16 = jnp.bfloat16
HIGHEST = lax.Precision.HIGHEST

D_MODEL = 1024
MLA_HEADS = 8
MLA_NOPE = 64
MLA_ROPE = 32
MLA_QK = MLA_NOPE + MLA_ROPE
MLA_V = 64
MLA_Q_RANK = 384
MLA_KV_RANK = 256
ROPE_BASE = 10000.0
GLA_HEADS = 4
GLA_DK = 64
GLA_DV = 128
GLA_GATE_RANK = 16
GLA_TAU = 16.0
GLA_CHUNK = 64
S5_CH = 512
S5_GROUP = 16
S5_GROUPS = S5_CH // S5_GROUP
S5_STATE = 64
N_BRANCH = 3
BRANCH_W = 512
D_FF = 4 * D_MODEL
EPS = 1e-6

LANES = 128
HEAD_SLOT = LANES
SUBLANES = 8
S5_T = 8
S5_GB = LANES // S5_GROUP
S5_NB = S5_GROUPS // S5_GB
S5_BW = S5_T * LANES
S5_SW = S5_GB * S5_STATE
VMEM_LIMIT = 56 * 1024 * 1024
NEG = -1e30
FLASH_TQ = 1024
TOKEN_TILE = 1024
FLASH_KEY_SLABS = 4

ZMLA_W = MLA_Q_RANK + MLA_KV_RANK + 2 * LANES
ZGLA_W = 2 * GLA_HEADS * GLA_DK + 2 * GLA_HEADS * GLA_DV + LANES
GATE_W = N_BRANCH * D_MODEL


def _cparams(sem):
    return pltpu.CompilerParams(dimension_semantics=sem, vmem_limit_bytes=VMEM_LIMIT)


def _const_spec(shape):
    nd = len(shape)
    return pl.BlockSpec(shape, lambda *_: (0,) * nd, pipeline_mode=pl.Buffered(1))


def _dot(a, b):
    return jnp.dot(a, b, preferred_element_type=F32)


def _dot_nt(a, b):
    return lax.dot_general(a, b, (((1,), (1,)), ((), ())), preferred_element_type=F32)


def _in_proj_kernel(x_ref, g_ref, w_ref, gb_ref, zmla_ref, zgla_ref, su_ref, gate_ref, h_ref, s_ref, *, chunk):
    x = x_ref[...]
    ms = jnp.mean(x * x, axis=-1, keepdims=True)
    h_ref[...] = (x * lax.rsqrt(ms + EPS) * g_ref[...]).astype(BF16)
    col = 0
    for out_ref in (zmla_ref, zgla_ref):
        width = out_ref.shape[-1]
        for c0 in range(0, width, chunk):
            c1 = min(c0 + chunk, width)
            out_ref[:, c0:c1] = _dot(h_ref[...], w_ref[:, col + c0:col + c1]).astype(BF16)
        col += width
    su = _dot(h_ref[...], w_ref[:, col:col + S5_CH])
    col += S5_CH
    rows = s_ref.shape[1] // S5_T
    for b in range(S5_NB):
        s_ref[b] = su[:, b * LANES:(b + 1) * LANES]
        for t in range(S5_T):
            su_ref[b, :, t * LANES:(t + 1) * LANES] = s_ref[b, pl.ds(t, rows, stride=S5_T), :].astype(BF16)
    for c0 in range(0, GATE_W, chunk):
        pre = _dot(h_ref[...], w_ref[:, col + c0:col + c0 + chunk]) + gb_ref[:, c0:c0 + chunk]
        gate_ref[:, c0:c0 + chunk] = jax.nn.sigmoid(pre).astype(BF16)


def _in_proj(xf, g, w, gate_b, tn):
    n = xf.shape[0]
    wtot = w.shape[1]
    return pl.pallas_call(
        functools.partial(_in_proj_kernel, chunk=512),
        grid=(n // tn,),
        in_specs=[
            pl.BlockSpec((tn, D_MODEL), lambda i: (i, 0)),
            _const_spec((1, D_MODEL)),
            _const_spec((D_MODEL, wtot)),
            _const_spec((1, GATE_W)),
        ],
        out_specs=[
            pl.BlockSpec((tn, ZMLA_W), lambda i: (i, 0)),
            pl.BlockSpec((tn, ZGLA_W), lambda i: (i, 0)),
            pl.BlockSpec((S5_NB, tn // S5_T, S5_BW), lambda i: (0, i, 0)),
            pl.BlockSpec((tn, GATE_W), lambda i: (i, 0)),
        ],
        out_shape=[
            jax.ShapeDtypeStruct((n, ZMLA_W), BF16),
            jax.ShapeDtypeStruct((n, ZGLA_W), BF16),
            jax.ShapeDtypeStruct((S5_NB, n // S5_T, S5_BW), BF16),
            jax.ShapeDtypeStruct((n, GATE_W), BF16),
        ],
        scratch_shapes=[pltpu.VMEM((tn, D_MODEL), BF16), pltpu.VMEM((S5_NB, tn, LANES), F32)],
        compiler_params=_cparams(("arbitrary",)),
        name="in_proj",
    )(xf, g, w, gate_b)


def _mla_prep_kernel(z_ref, cos_ref, sin_ref, gqn_ref, gkvn_ref, wq_ref, wqs_ref, wk_ref, wv_ref,
                     gq_ref, gqs_ref, gk_ref, gks_ref, vone_ref, q_ref, k_ref, v_ref):
    cq = z_ref[0, :, 0:MLA_Q_RANK].astype(F32)
    ckv = z_ref[0, :, MLA_Q_RANK:MLA_Q_RANK + MLA_KV_RANK].astype(F32)
    o = MLA_Q_RANK + MLA_KV_RANK
    kpe = z_ref[0, :, o:o + LANES].astype(F32)
    kpe_sw = z_ref[0, :, o + LANES:o + 2 * LANES].astype(F32)
    cos = cos_ref[...]
    sin = sin_ref[...]

    cqn = (cq * lax.rsqrt(jnp.mean(cq * cq, axis=-1, keepdims=True) + EPS) * gqn_ref[...]).astype(BF16)
    ckvn = (ckv * lax.rsqrt(jnp.mean(ckv * ckv, axis=-1, keepdims=True) + EPS) * gkvn_ref[...]).astype(BF16)

    q_raw = _dot(cqn, wq_ref[...])
    q_sw = _dot(cqn, wqs_ref[...])
    k_nope = _dot(ckvn, wk_ref[...])
    v_all = _dot(ckvn, wv_ref[...]) + vone_ref[...]

    ssq_pe = jnp.sum(kpe * kpe, axis=-1, keepdims=True)
    q_scale = MLA_QK ** -0.5 * math.log2(math.e)

    cq_t = gq_ref[...] * cos
    sq_t = gqs_ref[...] * sin
    ck_t = gk_ref[...] * cos
    kpe_rot = kpe * ck_t + kpe_sw * (gks_ref[...] * sin)
    for h in range(MLA_HEADS):
        sl = slice(h * HEAD_SLOT, (h + 1) * HEAD_SLOT)
        qr, kn = q_raw[:, sl], k_nope[:, sl]
        rq = lax.rsqrt(jnp.sum(qr * qr, axis=-1, keepdims=True) * (1.0 / MLA_QK) + EPS) * q_scale
        qh = rq * (qr * cq_t + q_sw[:, sl] * sq_t)
        q_ref[0, h] = qh.astype(BF16)
        rk = lax.rsqrt((jnp.sum(kn * kn, axis=-1, keepdims=True) + ssq_pe) * (1.0 / MLA_QK) + EPS)
        kh = rk * (kn * ck_t + kpe_rot)
        k_ref[0, h] = kh.astype(BF16)
        v_ref[0, h] = v_all[:, sl].T.astype(BF16)


def _mla_prep(zmla, cos128, sin128, p, tl):
    b, l, _ = zmla.shape
    hw = MLA_HEADS * HEAD_SLOT
    head_out = jax.ShapeDtypeStruct((b, MLA_HEADS, l, HEAD_SLOT), BF16)
    head_spec = pl.BlockSpec((1, MLA_HEADS, tl, HEAD_SLOT), lambda i, j: (i, 0, j, 0))
    return pl.pallas_call(
        _mla_prep_kernel,
        grid=(b, l // tl),
        in_specs=[
            pl.BlockSpec((1, tl, ZMLA_W), lambda i, j: (i, j, 0)),
            pl.BlockSpec((tl, LANES), lambda i, j: (j, 0)),
            pl.BlockSpec((tl, LANES), lambda i, j: (j, 0)),
            _const_spec((1, MLA_Q_RANK)),
            _const_spec((1, MLA_KV_RANK)),
            _const_spec((MLA_Q_RANK, hw)),
            _const_spec((MLA_Q_RANK, hw)),
            _const_spec((MLA_KV_RANK, hw)),
            _const_spec((MLA_KV_RANK, hw)),
            _const_spec((1, LANES)),
            _const_spec((1, LANES)),
            _const_spec((1, LANES)),
            _const_spec((1, LANES)),
            _const_spec((1, hw)),
        ],
        out_specs=[head_spec, head_spec,
                   pl.BlockSpec((1, MLA_HEADS, HEAD_SLOT, tl), lambda i, j: (i, 0, 0, j))],
        out_shape=[head_out, head_out, jax.ShapeDtypeStruct((b, MLA_HEADS, HEAD_SLOT, l), BF16)],
        compiler_params=_cparams(("arbitrary", "arbitrary")),
        name="mla_prep",
    )(zmla, cos128, sin128, p["gqn"], p["gkvn"], p["wq"], p["wqs"], p["wk"], p["wv"],
      p["gq"], p["gqs"], p["gk"], p["gks"], p["vone"])


def _flash_kernel(q_ref, k_ref, vt_ref, o_ref, *, tq):
    qi = pl.program_id(2)
    srow = lax.broadcasted_iota(jnp.int32, (HEAD_SLOT, tq), 0)

    def _causal(nk, nq, q_off):
        return (lax.broadcasted_iota(jnp.int32, (nk, nq), 0)
                <= lax.broadcasted_iota(jnp.int32, (nk, nq), 1) + q_off)

    def update(m, acc, s, vt):
        sb = s.astype(BF16)
        m_new = jnp.maximum(m, jnp.max(sb, axis=0, keepdims=True).astype(F32))
        alpha = jnp.exp2(m - m_new)
        p = jnp.exp2(sb - m_new.astype(BF16))
        return m_new, alpha * acc + _dot(vt, p)

    half = tq // 2

    kw = tq // FLASH_KEY_SLABS

    def step(j, carry):
        carry = [list(c) for c in carry]

        def slab_scores(hh, ks):
            start = pl.multiple_of(j * tq + ks * kw, kw)
            return _dot_nt(k_ref[0, hh, pl.ds(start, kw), :], q_ref[0, hh])

        pending = [slab_scores(hh, 0) for hh in range(2)]
        for ks in range(FLASH_KEY_SLABS):
            start = pl.multiple_of(j * tq + ks * kw, kw)
            for hh in range(2):
                s = pending[hh]
                if ks + 1 < FLASH_KEY_SLABS:
                    pending[hh] = slab_scores(hh, ks + 1)
                vt = vt_ref[0, hh, :, pl.ds(start, kw)]
                carry[hh] = [update(*carry[hh][0], s[:, :half], vt), update(*carry[hh][1], s[:, half:], vt)]
        return tuple(tuple(c) for c in carry)

    def diagonal(carry):
        start = pl.multiple_of(qi * tq, tq)
        s_lo = [_dot_nt(k_ref[0, hh, pl.ds(start, half), :], q_ref[0, hh, 0:half, :]) for hh in range(2)]
        s_hi = [_dot_nt(k_ref[0, hh, pl.ds(start, tq), :], q_ref[0, hh, half:, :]) for hh in range(2)]
        accs = []
        for hh in range(2):
            lo = jnp.where(_causal(half, half, 0), s_lo[hh], NEG)
            hi = jnp.where(_causal(tq, half, half), s_hi[hh], NEG)
            _, a_lo = update(*carry[hh][0], lo, vt_ref[0, hh, :, pl.ds(start, half)])
            _, a_hi = update(*carry[hh][1], hi, vt_ref[0, hh, :, pl.ds(start, tq)])
            accs.append(jnp.concatenate([a_lo, a_hi], axis=1))
        return accs

    init = (jnp.full((1, half), NEG, F32), jnp.zeros((HEAD_SLOT, half), F32))
    carry = lax.fori_loop(0, qi, step, ((init, init), (init, init)))
    acc0, acc1 = diagonal(carry)
    o0 = acc0 / acc0[MLA_V:MLA_V + 1, :]
    o1 = acc1 / acc1[0:1, :]
    o_ref[0] = jnp.where(srow < MLA_V, o0, o1).T.astype(BF16)


def _flash(q, k, vt, tq):
    b, h, l, _ = q.shape
    return pl.pallas_call(
        functools.partial(_flash_kernel, tq=tq),
        grid=(b, h // 2, l // tq),
        in_specs=[pl.BlockSpec((1, 2, tq, HEAD_SLOT), lambda i, j, t: (i, j, t, 0)),
                  pl.BlockSpec((1, 2, l, HEAD_SLOT), lambda i, j, t: (i, j, 0, 0)),
                  pl.BlockSpec((1, 2, HEAD_SLOT, l), lambda i, j, t: (i, j, 0, 0))],
        out_specs=pl.BlockSpec((1, tq, LANES), lambda i, j, t: (i, t, j)),
        out_shape=jax.ShapeDtypeStruct((b, l, h * MLA_V), BF16),
        compiler_params=_cparams(("arbitrary", "arbitrary", "arbitrary")),
        name="flash",
    )(q, k, vt)


GLA_QW = GLA_HEADS * GLA_DK
GLA_VW = GLA_HEADS * GLA_DV


def _split_bf16(a):
    hi = a.astype(BF16)
    lo = (a - hi.astype(F32)).astype(BF16)
    return hi, lo


def _gla_kernel(z_ref, wg_ref, bg_ref, og_ref, tri_ref, upp_ref, mk_ref, mv_ref, mvt_ref, causal_ref, bdn_ref,
                o_ref, st_ref, oacc_ref, *, tg):
    @pl.when(pl.program_id(0) == 0)
    def _():
        st_ref[...] = jnp.zeros_like(st_ref)

    nb = z_ref.shape[0]
    nc = tg // GLA_CHUNK
    rows = [slice(c * GLA_CHUNK, (c + 1) * GLA_CHUNK) for c in range(nc)]

    pre = [_dot(z_ref[b, :, 2 * GLA_QW + 2 * GLA_VW:], wg_ref[...]) + bg_ref[...] for b in range(nb)]
    bc, qt, kt, kend = [], [], [], []
    for b in range(nb):
        la = (jnp.minimum(pre[b], 0.0) - jnp.log(1.0 + jnp.exp(-jnp.abs(pre[b])))) * (1.0 / GLA_TAU)
        la_hi, la_lo = _split_bf16(la)
        bc.append(_dot(tri_ref[...], la_hi) + _dot(tri_ref[...], la_lo))
        rem = _dot(upp_ref[...], la_hi) + _dot(upp_ref[...], la_lo)
        q = z_ref[b, :, 0:GLA_QW].astype(F32)
        k = z_ref[b, :, GLA_QW:2 * GLA_QW].astype(F32)
        qt.append((q * (GLA_DK ** -0.5) * jnp.exp(bc[b])).astype(BF16))
        kt.append((k * jnp.exp(-bc[b])).astype(BF16))
        kend.append((k * jnp.exp(rem)).astype(BF16))

    v = [z_ref[b, :, 2 * GLA_QW:2 * GLA_QW + GLA_VW] for b in range(nb)]

    a = {}
    for b in range(nb):
        for c in range(nc):
            krows = jnp.concatenate([kt[b][rows[c]]] * GLA_HEADS, axis=0) * mk_ref[...]
            a[b, c] = _dot_nt(qt[b][rows[c]], krows)
    for b in range(nb):
        for c in range(nc):
            am = jnp.where(causal_ref[...] > 0, a[b, c], 0.0).astype(BF16)
            vbd = jnp.concatenate([v[b][rows[c]]] * GLA_HEADS, axis=0) * mv_ref[...]
            oacc_ref[b, rows[c], :] = _dot(am, vbd)

    def state_increments(c):
        return [_dot(v[b][rows[c]].astype(F32).T.astype(BF16), kend[b][rows[c]]) for b in range(nb)]

    dst = state_increments(0)
    for c in range(nc):
        dst_next = state_increments(c + 1) if c + 1 < nc else None
        for b in range(nb):
            st = st_ref[b]
            oacc_ref[b, rows[c], :] += _dot_nt(qt[b][rows[c]], st.astype(BF16) * mvt_ref[...])
            dec = jnp.exp(bc[b][(c + 1) * GLA_CHUNK - 1:(c + 1) * GLA_CHUNK, :])
            st_ref[b] = st * dec + dst[b]
        dst = dst_next

    for b in range(nb):
        o = oacc_ref[b]
        r = z_ref[b, :, 2 * GLA_QW + GLA_VW:2 * GLA_QW + 2 * GLA_VW].astype(F32)
        ss = _dot((o * o).astype(BF16), bdn_ref[...])
        y = o * lax.rsqrt(ss * (1.0 / GLA_DV) + EPS) * og_ref[...]
        o_ref[b] = (y * (r * jax.nn.sigmoid(r))).astype(BF16)


def _gla(zgla, p, c, tg):
    b, l, _ = zgla.shape
    return pl.pallas_call(
        functools.partial(_gla_kernel, tg=tg),
        grid=(l // tg,),
        in_specs=[
            pl.BlockSpec((b, tg, ZGLA_W), lambda j: (0, j, 0)),
            _const_spec((LANES, GLA_QW)),
            _const_spec((1, GLA_QW)),
            _const_spec((1, GLA_VW)),
            _const_spec((tg, tg)),
            _const_spec((tg, tg)),
            _const_spec((GLA_HEADS * GLA_CHUNK, GLA_QW)),
            _const_spec((GLA_HEADS * GLA_CHUNK, GLA_VW)),
            _const_spec((GLA_VW, GLA_QW)),
            _const_spec((GLA_CHUNK, GLA_HEADS * GLA_CHUNK)),
            _const_spec((GLA_VW, GLA_VW)),
        ],
        out_specs=pl.BlockSpec((b, tg, GLA_VW), lambda j: (0, j, 0)),
        out_shape=jax.ShapeDtypeStruct((b, l, GLA_VW), BF16),
        scratch_shapes=[pltpu.VMEM((b, GLA_VW, GLA_QW), F32), pltpu.VMEM((b, tg, GLA_VW), F32)],
        compiler_params=_cparams(("arbitrary",)),
        name="gla",
    )(zgla, p["w_gate"], p["b_gate"], p["gla_og"], c["gla_tri"], c["gla_upp"], c["gla_mk"], c["gla_mv"],
      c["gla_mvt"], c["gla_causal"], c["gla_bdn"])


def _s5_state_kernel(x_ref, p_ref, dre_ref, dim_ref):
    d = _dot(x_ref[0], p_ref[0])
    dre_ref[...] = d[:, :S5_SW]
    dim_ref[...] = d[:, S5_SW:]


def _s5_state(xb, p8, tr):
    nb, r, _ = xb.shape
    out = jax.ShapeDtypeStruct((r, nb * S5_SW), F32)
    ospec = pl.BlockSpec((tr, S5_SW), lambda i, j: (j, i))
    return pl.pallas_call(
        _s5_state_kernel,
        grid=(nb, r // tr),
        in_specs=[
            pl.BlockSpec((1, tr, S5_BW), lambda i, j: (i, j, 0)),
            pl.BlockSpec((1, S5_BW, 2 * S5_SW), lambda i, j: (i, 0, 0)),
        ],
        out_specs=[ospec, ospec],
        out_shape=[out, out],
        compiler_params=_cparams(("arbitrary", "arbitrary")),
        name="s5_state",
    )(xb, p8)


def _s5_scan_kernel(dre_ref, dim_ref, are_ref, aim_ref, sre_ref, sim_ref, wre_ref, wim_ref, cre_ref, cim_ref,
                    *, cb, nbatch):
    @pl.when(pl.program_id(0) == 0)
    def _():
        wre_ref[...] = jnp.zeros_like(wre_ref)
        wim_ref[...] = jnp.zeros_like(wim_ref)
        cre_ref[...] = jnp.zeros_like(cre_ref)
        cim_ref[...] = jnp.zeros_like(cim_ref)

    nt = wre_ref.shape[0]
    for b in range(nbatch):
        for j in range(nt):
            wre_ref[j, pl.ds(b, cb, stride=SUBLANES), :] = dre_ref[b, :, j * LANES:(j + 1) * LANES]
            wim_ref[j, pl.ds(b, cb, stride=SUBLANES), :] = dim_ref[b, :, j * LANES:(j + 1) * LANES]
    are = jnp.broadcast_to(are_ref[...], cre_ref.shape)
    aim = jnp.broadcast_to(aim_ref[...], cre_ref.shape)

    def body(c, carry):
        s_re, s_im = carry
        rows = pl.ds(pl.multiple_of(c * SUBLANES, SUBLANES), SUBLANES)
        d_re = wre_ref[:, rows, :]
        d_im = wim_ref[:, rows, :]
        wre_ref[:, rows, :] = s_re
        wim_ref[:, rows, :] = s_im
        return are * s_re - aim * s_im + d_re, are * s_im + aim * s_re + d_im

    s_re, s_im = lax.fori_loop(0, cb, body, (cre_ref[...], cim_ref[...]))
    cre_ref[...] = s_re
    cim_ref[...] = s_im
    for b in range(nbatch):
        for j in range(nt):
            sre_ref[b, :, j * LANES:(j + 1) * LANES] = wre_ref[j, pl.ds(b, cb, stride=SUBLANES), :]
            sim_ref[b, :, j * LANES:(j + 1) * LANES] = wim_ref[j, pl.ds(b, cb, stride=SUBLANES), :]


def _s5_scan(dre, dim, are, aim, cb):
    b, c, sw = dre.shape
    nt = sw // LANES
    spec = pl.BlockSpec((b, cb, sw), lambda i: (0, i, 0))
    out = jax.ShapeDtypeStruct((b, c, sw), F32)
    work = pltpu.VMEM((nt, cb * SUBLANES, LANES), F32)
    carry = pltpu.VMEM((nt, SUBLANES, LANES), F32)
    return pl.pallas_call(
        functools.partial(_s5_scan_kernel, cb=cb, nbatch=b),
        grid=(c // cb,),
        in_specs=[spec, spec, _const_spec((nt, 1, LANES)), _const_spec((nt, 1, LANES))],
        out_specs=[spec, spec],
        out_shape=[out, out],
        scratch_shapes=[work, work, carry, carry],
        compiler_params=_cparams(("arbitrary",)),
        name="s5_scan",
    )(dre, dim, are.reshape(nt, 1, LANES), aim.reshape(nt, 1, LANES))


def _gelu_tanh(y):
    return 0.5 * y * (1.0 + jnp.tanh(math.sqrt(2.0 / math.pi) * (y + 0.044715 * (y * y * y))))


def _s5_out_kernel(x_ref, sre_ref, sim_ref, m_ref, q_ref, y_ref):
    s8 = jnp.concatenate([sre_ref[...], sim_ref[...]], axis=1).astype(BF16)
    y = _dot(x_ref[0], m_ref[0]) + _dot(s8, q_ref[0])
    y_ref[0] = _gelu_tanh(y).astype(BF16)


def _s5_out(xb, sre, sim, m8, q8, tr):
    nb, r, _ = xb.shape
    sspec = pl.BlockSpec((tr, S5_SW), lambda i, j: (j, i))
    xspec = pl.BlockSpec((1, tr, S5_BW), lambda i, j: (i, j, 0))
    return pl.pallas_call(
        _s5_out_kernel,
        grid=(nb, r // tr),
        in_specs=[
            xspec, sspec, sspec,
            pl.BlockSpec((1, S5_BW, S5_BW), lambda i, j: (i, 0, 0)),
            pl.BlockSpec((1, 2 * S5_SW, S5_BW), lambda i, j: (i, 0, 0)),
        ],
        out_specs=xspec,
        out_shape=jax.ShapeDtypeStruct((nb, r, S5_BW), BF16),
        compiler_params=_cparams(("arbitrary", "arbitrary")),
        name="s5_out",
    )(xb, sre, sim, m8, q8)


def _s5_params(lam_re, lam_im, b_re, b_im, c_re, c_im, d, log_dt):
    g = S5_GROUPS
    lre = jnp.minimum(lam_re.astype(F32), -1e-4)
    lim = lam_im.astype(F32)
    step = jnp.exp(log_dt.astype(F32))[:, None]
    pw = jnp.arange(S5_T + 1, dtype=F32)[:, None, None]
    mag = jnp.exp(pw * (lre * step)[None])
    ang = pw * (lim * step)[None]
    pre, pim = mag * jnp.cos(ang), mag * jnp.sin(ang)
    nr, ni = pre[1] - 1.0, pim[1]
    den = lre * lre + lim * lim
    cr = (nr * lre + ni * lim) / den
    ci = (ni * lre - nr * lim) / den
    bre = cr[..., None] * b_re - ci[..., None] * b_im
    bim = cr[..., None] * b_im + ci[..., None] * b_re
    lbr = pre[..., None] * bre[None] - pim[..., None] * bim[None]
    lbi = pre[..., None] * bim[None] + pim[..., None] * bre[None]
    kd = (jnp.einsum("gip,dgpj->dgij", c_re, lbr, precision=HIGHEST)
          - jnp.einsum("gip,dgpj->dgij", c_im, lbi, precision=HIGHEST))
    lag = np.arange(S5_T)[None, :] - np.arange(S5_T)[:, None]
    kst = kd[np.clip(lag, 0, S5_T)]
    kst = jnp.where((lag >= 0)[:, :, None, None, None], kst, 0.0)
    m = kst.transpose(2, 0, 4, 1, 3)
    eye_t = jnp.eye(S5_T, dtype=F32)[None, :, None, :, None]
    eye_i = jnp.eye(S5_GROUP, dtype=F32)[None, None, :, None, :]
    m = m + eye_t * eye_i * d.astype(F32)[:, None, None, None, :]
    rev = np.arange(S5_T - 1, -1, -1)
    p_re = lbr[rev].transpose(1, 0, 3, 2)
    p_im = lbi[rev].transpose(1, 0, 3, 2)
    qr = c_re[None] * pre[1:, :, None, :] - c_im[None] * pim[1:, :, None, :]
    qi = -(c_re[None] * pim[1:, :, None, :] + c_im[None] * pre[1:, :, None, :])
    q_re = qr.transpose(1, 3, 0, 2)
    q_im = qi.transpose(1, 3, 0, 2)

    nb, gb = S5_NB, S5_GB
    lanes_ti = np.arange(S5_BW) // LANES * S5_GROUP + np.arange(S5_BW) % S5_GROUP
    exp_ti = jnp.asarray(np.arange(LANES)[:, None] == lanes_ti[None, :], BF16)
    exp_n = jnp.asarray(np.arange(S5_STATE)[:, None] == (np.arange(S5_SW) % S5_STATE)[None, :], BF16)
    g_row_sgj = np.arange(S5_BW) // S5_GROUP % gb
    g_col_tgi = np.arange(S5_BW) % LANES // S5_GROUP
    g_gn = np.arange(S5_SW) // S5_STATE

    def expand(compact, expander, row_g, col_g):
        full = jnp.einsum("brk,kc->brc", compact.astype(BF16), expander, preferred_element_type=BF16)
        return full * jnp.asarray(row_g[:, None] == col_g[None, :], BF16)

    def rows_sgj(a):
        return a.reshape(nb, gb, S5_T, S5_GROUP, -1).transpose(0, 2, 1, 3, 4).reshape(nb, S5_BW, -1)

    m8 = expand(rows_sgj(m.reshape(g, S5_T, S5_GROUP, LANES)), exp_ti, g_row_sgj, g_col_tgi)
    p8 = jnp.concatenate([expand(rows_sgj(p), exp_n, g_row_sgj, g_gn) for p in (p_re, p_im)], axis=2)
    q8 = jnp.concatenate([expand(q.reshape(nb, S5_SW, LANES), exp_ti, g_gn, g_col_tgi) for q in (q_re, q_im)],
                         axis=1)
    are = pre[S5_T].reshape(1, g * S5_STATE)
    aim = pim[S5_T].reshape(1, g * S5_STATE)
    return m8, p8, q8, are, aim


def _s5(xb, sp, b, l):
    m8, p8, q8, are, aim = sp
    c = l // S5_T
    r = b * c
    tr = min(1024, r)
    dre, dim = _s5_state(xb, p8, tr)
    sw = dre.shape[-1]
    sre, sim = _s5_scan(dre.reshape(b, c, sw), dim.reshape(b, c, sw), are, aim, min(64, c))
    return _s5_out(xb, sre.reshape(r, sw), sim.reshape(r, sw), m8, q8, tr)


def _merge_kernel(x_ref, oa_ref, ob_ref, y_ref, g_ref, wa_ref, wb_ref, wc_ref, wglu_ref, bglu_ref, wout_ref,
                  o_ref, ys_ref):
    rows = y_ref.shape[1]
    for b in range(S5_NB):
        for t in range(S5_T):
            ys_ref[b, pl.ds(t, rows, stride=S5_T), :] = y_ref[b, :, t * LANES:(t + 1) * LANES].astype(F32)
    y = jnp.concatenate([ys_ref[b] for b in range(S5_NB)], axis=1)
    oc = (y * jax.nn.sigmoid(_dot(y.astype(BF16), wglu_ref[...]) + bglu_ref[...])).astype(BF16)
    merged = (g_ref[:, 0:D_MODEL].astype(F32) * _dot(oa_ref[...], wa_ref[...])
              + g_ref[:, D_MODEL:2 * D_MODEL].astype(F32) * _dot(ob_ref[...], wb_ref[...])
              + g_ref[:, 2 * D_MODEL:].astype(F32) * _dot(oc, wc_ref[...]))
    o_ref[...] = x_ref[...] + _dot(merged.astype(BF16), wout_ref[...])


def _merge(xf, oa, ob, y, gates, p, tn):
    n = xf.shape[0]
    row = lambda w: pl.BlockSpec((tn, w), lambda i: (i, 0))
    return pl.pallas_call(
        _merge_kernel,
        grid=(n // tn,),
        in_specs=[
            row(D_MODEL), row(BRANCH_W), row(BRANCH_W),
            pl.BlockSpec((S5_NB, tn // S5_T, S5_BW), lambda i: (0, i, 0)), row(GATE_W),
            _const_spec((BRANCH_W, D_MODEL)), _const_spec((BRANCH_W, D_MODEL)), _const_spec((BRANCH_W, D_MODEL)),
            _const_spec((S5_CH, S5_CH)), _const_spec((1, S5_CH)), _const_spec((D_MODEL, D_MODEL)),
        ],
        out_specs=row(D_MODEL),
        out_shape=jax.ShapeDtypeStruct((n, D_MODEL), F32),
        scratch_shapes=[pltpu.VMEM((S5_NB, tn, LANES), F32)],
        compiler_params=_cparams(("arbitrary",)),
        name="merge",
    )(xf, oa, ob, y, gates, p["w_br_mla"], p["w_br_gla"], p["w_br_s5"], p["w_glu"], p["b_glu"], p["w_out"])


def _ffn_kernel(x_ref, g_ref, w1_ref, w2_ref, o_ref, h_ref, *, chunk):
    x = x_ref[...]
    ms = jnp.mean(x * x, axis=-1, keepdims=True)
    h_ref[...] = (x * lax.rsqrt(ms + EPS) * g_ref[...]).astype(BF16)
    o_ref[...] = x
    for c0 in range(0, D_FF, chunk):
        a = jnp.maximum(_dot(h_ref[...], w1_ref[:, c0:c0 + chunk]), 0.0)
        o_ref[...] += _dot((a * a).astype(BF16), w2_ref[c0:c0 + chunk, :])


def _ffn(xf, g, w1, w2, tn):
    n = xf.shape[0]
    return pl.pallas_call(
        functools.partial(_ffn_kernel, chunk=512),
        grid=(n // tn,),
        in_specs=[
            pl.BlockSpec((tn, D_MODEL), lambda i: (i, 0)),
            _const_spec((1, D_MODEL)),
            _const_spec((D_MODEL, D_FF)),
            _const_spec((D_FF, D_MODEL)),
        ],
        out_specs=pl.BlockSpec((tn, D_MODEL), lambda i: (i, 0)),
        out_shape=jax.ShapeDtypeStruct((n, D_MODEL), F32),
        scratch_shapes=[pltpu.VMEM((tn, D_MODEL), BF16)],
        compiler_params=_cparams(("arbitrary",)),
        name="ffn",
    )(xf, g, w1, w2)


def _head_slots(w, width):
    k = w.shape[0]
    w = w.reshape(k, MLA_HEADS, width)
    return jnp.pad(w, ((0, 0), (0, 0), (0, HEAD_SLOT - width))).reshape(k, MLA_HEADS * HEAD_SLOT)


def _swap_rope_halves(a):
    half = MLA_ROPE // 2
    return jnp.concatenate([jnp.zeros_like(a[..., :MLA_NOPE]), a[..., MLA_NOPE + half:], a[..., MLA_NOPE:MLA_NOPE + half]],
                           axis=-1)


def _constants(tg):
    c = {}
    t = np.arange(tg)
    same = (t[:, None] // GLA_CHUNK) == (t[None, :] // GLA_CHUNK)
    c["gla_tri"] = jnp.asarray(same & (t[None, :] <= t[:, None]), BF16)
    c["gla_upp"] = jnp.asarray(same & (t[None, :] > t[:, None]), BF16)
    hs = np.arange(GLA_HEADS * GLA_CHUNK) // GLA_CHUNK
    hk = np.arange(GLA_QW) // GLA_DK
    hv = np.arange(GLA_VW) // GLA_DV
    c["gla_mk"] = jnp.asarray(hs[:, None] == hk[None, :], BF16)
    c["gla_mv"] = jnp.asarray(hs[:, None] == hv[None, :], BF16)
    c["gla_mvt"] = jnp.asarray(hv[:, None] == hk[None, :], BF16)
    s_in = np.arange(GLA_HEADS * GLA_CHUNK) % GLA_CHUNK
    c["gla_causal"] = jnp.asarray(s_in[None, :] <= np.arange(GLA_CHUNK)[:, None], F32)
    c["gla_bdn"] = jnp.asarray(np.kron(np.eye(GLA_HEADS), np.ones((GLA_DV, GLA_DV))), BF16)
    return c


def _rope_tables(l):
    pos = jnp.arange(l, dtype=F32)
    inv_freq = ROPE_BASE ** (-jnp.arange(0, MLA_ROPE, 2, dtype=F32) / MLA_ROPE)
    ang = pos[:, None] * inv_freq[None, :]
    cos, sin = jnp.cos(ang), jnp.sin(ang)
    pad = jnp.zeros((l, LANES - MLA_QK), F32)
    cos128 = jnp.concatenate([jnp.ones((l, MLA_NOPE), F32), cos, cos, pad], axis=-1)
    sin128 = jnp.concatenate([jnp.zeros((l, MLA_NOPE), F32), -sin, sin, pad], axis=-1)
    return cos128, sin128


def _layer_params(lyr, w_in, gate_b, mla_q_norm_g, mla_w_uq, mla_kv_norm_g, mla_w_ukv, mla_q_head_g,
                  mla_k_head_g, gla_w_gate, gla_b_gate, gla_out_g, s5_w_glu, s5_b_glu, w_br_mla, w_br_gla,
                  w_br_s5, w_out):
    p = {}
    w = w_in[lyr]
    sizes = (MLA_Q_RANK, MLA_KV_RANK, MLA_ROPE, GLA_QW, GLA_QW, GLA_VW, GLA_GATE_RANK, GLA_VW, S5_CH, GATE_W)
    offs = np.concatenate([[0], np.cumsum(sizes)])
    cq, ckv, kpe, gq, gk, gv, glr, gr, su, gates = [w[:, offs[i]:offs[i + 1]] for i in range(len(sizes))]
    half = MLA_ROPE // 2
    z = lambda n: jnp.zeros((D_MODEL, n), w.dtype)
    kpe_sw = jnp.concatenate([kpe[:, half:], kpe[:, :half]], axis=1)
    p["w_in"] = jnp.concatenate(
        [cq, ckv, z(MLA_NOPE), kpe, z(LANES - MLA_QK), z(MLA_NOPE), kpe_sw, z(LANES - MLA_QK),
         gq, gk, gv, gr, glr, z(LANES - GLA_GATE_RANK), su, gates], axis=1).astype(BF16)
    p["gate_b"] = gate_b[lyr].reshape(1, GATE_W)

    p["gqn"] = mla_q_norm_g[lyr].reshape(1, MLA_Q_RANK)
    p["gkvn"] = mla_kv_norm_g[lyr].reshape(1, MLA_KV_RANK)
    wuq = mla_w_uq[lyr].reshape(MLA_Q_RANK, MLA_HEADS, MLA_QK)
    p["wq"] = _head_slots(wuq.reshape(MLA_Q_RANK, -1), MLA_QK).astype(BF16)
    p["wqs"] = _head_slots(_swap_rope_halves(wuq).reshape(MLA_Q_RANK, -1), MLA_QK).astype(BF16)
    wukv = mla_w_ukv[lyr].reshape(MLA_KV_RANK, MLA_HEADS, MLA_NOPE + MLA_V)
    p["wk"] = _head_slots(wukv[..., :MLA_NOPE].reshape(MLA_KV_RANK, -1), MLA_NOPE).astype(BF16)
    wv = wukv[..., MLA_NOPE:]
    zv = jnp.zeros_like(wv)
    even = (np.arange(MLA_HEADS) % 2 == 0)[None, :, None]
    p["wv"] = jnp.where(even, jnp.concatenate([wv, zv], -1), jnp.concatenate([zv, wv], -1)).reshape(
        MLA_KV_RANK, MLA_HEADS * HEAD_SLOT).astype(BF16)
    vone = np.zeros((MLA_HEADS, HEAD_SLOT), np.float32)
    vone[0::2, MLA_V] = 1.0
    vone[1::2, 0] = 1.0
    p["vone"] = jnp.asarray(vone.reshape(1, -1))
    pad = lambda g: jnp.pad(g, (0, LANES - MLA_QK)).reshape(1, LANES)
    p["gq"] = pad(mla_q_head_g[lyr])
    p["gqs"] = pad(_swap_rope_halves(mla_q_head_g[lyr]))
    p["gk"] = pad(mla_k_head_g[lyr])
    p["gks"] = pad(_swap_rope_halves(mla_k_head_g[lyr]))

    p["w_gate"] = jnp.pad(gla_w_gate[lyr], ((0, LANES - GLA_GATE_RANK), (0, 0))).astype(BF16)
    p["b_gate"] = gla_b_gate[lyr].reshape(1, GLA_QW)
    p["gla_og"] = jnp.tile(gla_out_g[lyr], GLA_HEADS).reshape(1, GLA_VW)

    p["w_glu"] = s5_w_glu[lyr].astype(BF16)
    p["b_glu"] = s5_b_glu[lyr].reshape(1, S5_CH)
    p["w_br_mla"] = w_br_mla[lyr].astype(BF16)
    p["w_br_gla"] = w_br_gla[lyr].astype(BF16)
    p["w_br_s5"] = w_br_s5[lyr].astype(BF16)
    p["w_out"] = w_out[lyr].astype(BF16)
    return p


def kernel(x, norm1_g, w_in, mla_q_norm_g, mla_w_uq, mla_kv_norm_g, mla_w_ukv, mla_q_head_g, mla_k_head_g,
           gla_w_gate, gla_b_gate, gla_out_g, s5_lam_re, s5_lam_im, s5_b_re, s5_b_im, s5_c_re, s5_c_im, s5_d,
           s5_log_dt, s5_w_glu, s5_b_glu, w_br_mla, w_br_gla, w_br_s5, gate_b, w_out, norm2_g, w_ff1, w_ff2):
    b, l, d = x.shape
    n = b * l
    depth = w_in.shape[0]
    tn = min(TOKEN_TILE, n)
    tl = min(TOKEN_TILE, l)
    tg = min(256, l)
    cos128, sin128 = _rope_tables(l)
    consts = _constants(tg)
    xf = x.reshape(n, d)
    for lyr in range(depth):
        p = _layer_params(lyr, w_in, gate_b, mla_q_norm_g, mla_w_uq, mla_kv_norm_g, mla_w_ukv, mla_q_head_g,
                          mla_k_head_g, gla_w_gate, gla_b_gate, gla_out_g, s5_w_glu, s5_b_glu, w_br_mla,
                          w_br_gla, w_br_s5, w_out)
        sp = _s5_params(s5_lam_re[lyr], s5_lam_im[lyr], s5_b_re[lyr], s5_b_im[lyr], s5_c_re[lyr], s5_c_im[lyr],
                        s5_d[lyr], s5_log_dt[lyr])
        zmla, zgla, su, gates = _in_proj(xf, norm1_g[lyr].reshape(1, d), p["w_in"], p["gate_b"], tn)
        q, k, v = _mla_prep(zmla.reshape(b, l, ZMLA_W), cos128, sin128, p, tl)
        oa = _flash(q, k, v, min(FLASH_TQ, l)).reshape(n, BRANCH_W)
        ob = _gla(zgla.reshape(b, l, ZGLA_W), p, consts, tg).reshape(n, BRANCH_W)
        y = _s5(su, sp, b, l)
        x1 = _merge(xf, oa, ob, y, gates, p, tn)
        xf = _ffn(x1, norm2_g[lyr].reshape(1, d), w_ff1[lyr].astype(BF16), w_ff2[lyr].astype(BF16), tn)
    return xf.reshape(b, l, d)
```

```python
import functools
import math

import numpy as np
import jax
import jax.numpy as jnp
from jax import lax
from jax.experimental import pallas as pl
from jax.experimental.pallas import tpu as pltpu

F32 = jnp.float32
BF16 = jnp.bfloat16
HIGHEST = lax.Precision.HIGHEST

D_MODEL = 1024
MLA_HEADS = 8
MLA_NOPE = 64
MLA_ROPE = 32
MLA_QK = MLA_NOPE + MLA_ROPE
MLA_V = 64
MLA_Q_RANK = 384
MLA_KV_RANK = 256
ROPE_BASE = 10000.0
GLA_HEADS = 4
GLA_DK = 64
GLA_DV = 128
GLA_GATE_RANK = 16
GLA_TAU = 16.0
GLA_CHUNK = 64
S5_CH = 512
S5_GROUP = 16
S5_GROUPS = S5_CH // S5_GROUP
S5_STATE = 64
N_BRANCH = 3
BRANCH_W = 512
D_FF = 4 * D_MODEL
EPS = 1e-6

LANES = 128
HEAD_SLOT = LANES
SUBLANES = 8
S5_T = 8
S5_GB = LANES // S5_GROUP
S5_NB = S5_GROUPS // S5_GB
S5_BW = S5_T * LANES
S5_SW = S5_GB * S5_STATE
VMEM_LIMIT = 56 * 1024 * 1024
NEG = -1e30
FLASH_TQ = 1024
TOKEN_TILE = 1024
FLASH_KEY_SLABS = 4
FLASH_TILES_PER_STEP = 2

ZMLA_W = MLA_Q_RANK + MLA_KV_RANK + 2 * LANES
ZGLA_W = 2 * GLA_HEADS * GLA_DK + 2 * GLA_HEADS * GLA_DV + LANES
GATE_W = N_BRANCH * D_MODEL


def _cparams(sem):
    return pltpu.CompilerParams(dimension_semantics=sem, vmem_limit_bytes=VMEM_LIMIT)


def _const_spec(shape):
    nd = len(shape)
    return pl.BlockSpec(shape, lambda *_: (0,) * nd, pipeline_mode=pl.Buffered(1))


def _dot(a, b):
    return jnp.dot(a, b, preferred_element_type=F32)


def _dot_nt(a, b):
    return lax.dot_general(a, b, (((1,), (1,)), ((), ())), preferred_element_type=F32)


def _in_proj_kernel(x_ref, g_ref, w_ref, gb_ref, zmla_ref, zgla_ref, su_ref, gate_ref, h_ref, s_ref, *, chunk):
    x = x_ref[...]
    ms = jnp.mean(x * x, axis=-1, keepdims=True)
    h_ref[...] = (x * lax.rsqrt(ms + EPS) * g_ref[...]).astype(BF16)
    col = 0
    for out_ref in (zmla_ref, zgla_ref):
        width = out_ref.shape[-1]
        for c0 in range(0, width, chunk):
            c1 = min(c0 + chunk, width)
            out_ref[:, c0:c1] = _dot(h_ref[...], w_ref[:, col + c0:col + c1]).astype(BF16)
        col += width
    su = _dot(h_ref[...], w_ref[:, col:col + S5_CH])
    col += S5_CH
    rows = s_ref.shape[1] // S5_T
    for b in range(S5_NB):
        s_ref[b] = su[:, b * LANES:(b + 1) * LANES]
        for t in range(S5_T):
            su_ref[b, :, t * LANES:(t + 1) * LANES] = s_ref[b, pl.ds(t, rows, stride=S5_T), :].astype(BF16)
    for c0 in range(0, GATE_W, chunk):
        pre = _dot(h_ref[...], w_ref[:, col + c0:col + c0 + chunk]) + gb_ref[:, c0:c0 + chunk]
        gate_ref[:, c0:c0 + chunk] = jax.nn.sigmoid(pre).astype(BF16)


def _in_proj(xf, g, w, gate_b, tn):
    n = xf.shape[0]
    wtot = w.shape[1]
    return pl.pallas_call(
        functools.partial(_in_proj_kernel, chunk=512),
        grid=(n // tn,),
        in_specs=[
            pl.BlockSpec((tn, D_MODEL), lambda i: (i, 0)),
            _const_spec((1, D_MODEL)),
            _const_spec((D_MODEL, wtot)),
            _const_spec((1, GATE_W)),
        ],
        out_specs=[
            pl.BlockSpec((tn, ZMLA_W), lambda i: (i, 0)),
            pl.BlockSpec((tn, ZGLA_W), lambda i: (i, 0)),
            pl.BlockSpec((S5_NB, tn // S5_T, S5_BW), lambda i: (0, i, 0)),
            pl.BlockSpec((tn, GATE_W), lambda i: (i, 0)),
        ],
        out_shape=[
            jax.ShapeDtypeStruct((n, ZMLA_W), BF16),
            jax.ShapeDtypeStruct((n, ZGLA_W), BF16),
            jax.ShapeDtypeStruct((S5_NB, n // S5_T, S5_BW), BF16),
            jax.ShapeDtypeStruct((n, GATE_W), BF16),
        ],
        scratch_shapes=[pltpu.VMEM((tn, D_MODEL), BF16), pltpu.VMEM((S5_NB, tn, LANES), F32)],
        compiler_params=_cparams(("arbitrary",)),
        name="in_proj",
    )(xf, g, w, gate_b)


def _mla_prep_kernel(z_ref, cos_ref, sin_ref, gqn_ref, gkvn_ref, wq_ref, wqs_ref, wk_ref, wv_ref,
                     gq_ref, gqs_ref, gk_ref, gks_ref, vone_ref, q_ref, k_ref, v_ref):
    cq = z_ref[0, :, 0:MLA_Q_RANK].astype(F32)
    ckv = z_ref[0, :, MLA_Q_RANK:MLA_Q_RANK + MLA_KV_RANK].astype(F32)
    o = MLA_Q_RANK + MLA_KV_RANK
    kpe = z_ref[0, :, o:o + LANES].astype(F32)
    kpe_sw = z_ref[0, :, o + LANES:o + 2 * LANES].astype(F32)
    cos = cos_ref[...]
    sin = sin_ref[...]

    cqn = (cq * lax.rsqrt(jnp.mean(cq * cq, axis=-1, keepdims=True) + EPS) * gqn_ref[...]).astype(BF16)
    ckvn = (ckv * lax.rsqrt(jnp.mean(ckv * ckv, axis=-1, keepdims=True) + EPS) * gkvn_ref[...]).astype(BF16)

    q_raw = _dot(cqn, wq_ref[...])
    q_sw = _dot(cqn, wqs_ref[...])
    k_nope = _dot(ckvn, wk_ref[...])
    v_all = _dot(ckvn, wv_ref[...]) + vone_ref[...]

    ssq_pe = jnp.sum(kpe * kpe, axis=-1, keepdims=True)
    q_scale = MLA_QK ** -0.5 * math.log2(math.e)

    cq_t = gq_ref[...] * cos
    sq_t = gqs_ref[...] * sin
    ck_t = gk_ref[...] * cos
    kpe_rot = kpe * ck_t + kpe_sw * (gks_ref[...] * sin)
    for h in range(MLA_HEADS):
        sl = slice(h * HEAD_SLOT, (h + 1) * HEAD_SLOT)
        qr, kn = q_raw[:, sl], k_nope[:, sl]
        rq = lax.rsqrt(jnp.sum(qr * qr, axis=-1, keepdims=True) * (1.0 / MLA_QK) + EPS) * q_scale
        qh = rq * (qr * cq_t + q_sw[:, sl] * sq_t)
        q_ref[0, h] = qh.astype(BF16)
        rk = lax.rsqrt((jnp.sum(kn * kn, axis=-1, keepdims=True) + ssq_pe) * (1.0 / MLA_QK) + EPS)
        kh = rk * (kn * ck_t + kpe_rot)
        k_ref[0, h] = kh.astype(BF16)
        v_ref[0, h] = v_all[:, sl].T.astype(BF16)


def _mla_prep(zmla, cos128, sin128, p, tl):
    b, l, _ = zmla.shape
    hw = MLA_HEADS * HEAD_SLOT
    head_out = jax.ShapeDtypeStruct((b, MLA_HEADS, l, HEAD_SLOT), BF16)
    head_spec = pl.BlockSpec((1, MLA_HEADS, tl, HEAD_SLOT), lambda i, j: (i, 0, j, 0))
    return pl.pallas_call(
        _mla_prep_kernel,
        grid=(b, l // tl),
        in_specs=[
            pl.BlockSpec((1, tl, ZMLA_W), lambda i, j: (i, j, 0)),
            pl.BlockSpec((tl, LANES), lambda i, j: (j, 0)),
            pl.BlockSpec((tl, LANES), lambda i, j: (j, 0)),
            _const_spec((1, MLA_Q_RANK)),
            _const_spec((1, MLA_KV_RANK)),
            _const_spec((MLA_Q_RANK, hw)),
            _const_spec((MLA_Q_RANK, hw)),
            _const_spec((MLA_KV_RANK, hw)),
            _const_spec((MLA_KV_RANK, hw)),
            _const_spec((1, LANES)),
            _const_spec((1, LANES)),
            _const_spec((1, LANES)),
            _const_spec((1, LANES)),
            _const_spec((1, hw)),
        ],
        out_specs=[head_spec, head_spec,
                   pl.BlockSpec((1, MLA_HEADS, HEAD_SLOT, tl), lambda i, j: (i, 0, 0, j))],
        out_shape=[head_out, head_out, jax.ShapeDtypeStruct((b, MLA_HEADS, HEAD_SLOT, l), BF16)],
        compiler_params=_cparams(("arbitrary", "arbitrary")),
        name="mla_prep",
    )(zmla, cos128, sin128, p["gqn"], p["gkvn"], p["wq"], p["wqs"], p["wk"], p["wv"],
      p["gq"], p["gqs"], p["gk"], p["gks"], p["vone"])


def _flash_kernel(q_ref, k_ref, vt_ref, o_ref, *, tq):
    tiles = q_ref.shape[2] // tq
    for t in range(tiles):
        rows = slice(t * tq, (t + 1) * tq)
        _flash_tile(pl.program_id(2) * tiles + t, q_ref.at[:, :, rows, :], k_ref, vt_ref,
                    o_ref.at[:, rows, :], tq)


def _flash_tile(qi, q_ref, k_ref, vt_ref, o_ref, tq):
    srow = lax.broadcasted_iota(jnp.int32, (HEAD_SLOT, tq), 0)

    def _causal(nk, nq, q_off):
        return (lax.broadcasted_iota(jnp.int32, (nk, nq), 0)
                <= lax.broadcasted_iota(jnp.int32, (nk, nq), 1) + q_off)

    def update(m, acc, s, vt):
        sb = s.astype(BF16)
        m_new = jnp.maximum(m, jnp.max(sb, axis=0, keepdims=True).astype(F32))
        alpha = jnp.exp2(m - m_new)
        p = jnp.exp2(sb - m_new.astype(BF16))
        return m_new, alpha * acc + _dot(vt, p)

    half = tq // 2

    kw = tq // FLASH_KEY_SLABS

    def step(j, carry):
        carry = [list(c) for c in carry]

        def slab_scores(hh, ks):
            start = pl.multiple_of(j * tq + ks * kw, kw)
            return _dot_nt(k_ref[0, hh, pl.ds(start, kw), :], q_ref[0, hh])

        pending = [slab_scores(hh, 0) for hh in range(2)]
        for ks in range(FLASH_KEY_SLABS):
            start = pl.multiple_of(j * tq + ks * kw, kw)
            for hh in range(2):
                s = pending[hh]
                if ks + 1 < FLASH_KEY_SLABS:
                    pending[hh] = slab_scores(hh, ks + 1)
                vt = vt_ref[0, hh, :, pl.ds(start, kw)]
                carry[hh] = [update(*carry[hh][0], s[:, :half], vt), update(*carry[hh][1], s[:, half:], vt)]
        return tuple(tuple(c) for c in carry)

    def diagonal(carry):
        start = pl.multiple_of(qi * tq, tq)
        s_lo = [_dot_nt(k_ref[0, hh, pl.ds(start, half), :], q_ref[0, hh, 0:half, :]) for hh in range(2)]
        s_hi = [_dot_nt(k_ref[0, hh, pl.ds(start, tq), :], q_ref[0, hh, half:, :]) for hh in range(2)]
        accs = []
        for hh in range(2):
            lo = jnp.where(_causal(half, half, 0), s_lo[hh], NEG)
            hi = jnp.where(_causal(tq, half, half), s_hi[hh], NEG)
            _, a_lo = update(*carry[hh][0], lo, vt_ref[0, hh, :, pl.ds(start, half)])
            _, a_hi = update(*carry[hh][1], hi, vt_ref[0, hh, :, pl.ds(start, tq)])
            accs.append(jnp.concatenate([a_lo, a_hi], axis=1))
        return accs

    init = (jnp.full((1, half), NEG, F32), jnp.zeros((HEAD_SLOT, half), F32))
    carry = lax.fori_loop(0, qi, step, ((init, init), (init, init)))
    acc0, acc1 = diagonal(carry)
    o0 = acc0 / acc0[MLA_V:MLA_V + 1, :]
    o1 = acc1 / acc1[0:1, :]
    o_ref[0] = jnp.where(srow < MLA_V, o0, o1).T.astype(BF16)


def _flash(q, k, vt, tq):
    b, h, l, _ = q.shape
    ts = tq * FLASH_TILES_PER_STEP if l % (tq * FLASH_TILES_PER_STEP) == 0 else tq
    return pl.pallas_call(
        functools.partial(_flash_kernel, tq=tq),
        grid=(b, h // 2, l // ts),
        in_specs=[pl.BlockSpec((1, 2, ts, HEAD_SLOT), lambda i, j, t: (i, j, t, 0)),
                  pl.BlockSpec((1, 2, l, HEAD_SLOT), lambda i, j, t: (i, j, 0, 0)),
                  pl.BlockSpec((1, 2, HEAD_SLOT, l), lambda i, j, t: (i, j, 0, 0))],
        out_specs=pl.BlockSpec((1, ts, LANES), lambda i, j, t: (i, t, j)),
        out_shape=jax.ShapeDtypeStruct((b, l, h * MLA_V), BF16),
        compiler_params=_cparams(("arbitrary", "arbitrary", "arbitrary")),
        name="flash",
    )(q, k, vt)


GLA_QW = GLA_HEADS * GLA_DK
GLA_VW = GLA_HEADS * GLA_DV


def _split_bf16(a):
    hi = a.astype(BF16)
    lo = (a - hi.astype(F32)).astype(BF16)
    return hi, lo


def _gla_kernel(z_ref, wg_ref, bg_ref, og_ref, tri_ref, upp_ref, mk_ref, mv_ref, mvt_ref, causal_ref, bdn_ref,
                o_ref, st_ref, oacc_ref, *, tg):
    @pl.when(pl.program_id(0) == 0)
    def _():
        st_ref[...] = jnp.zeros_like(st_ref)

    nb = z_ref.shape[0]
    nc = tg // GLA_CHUNK
    rows = [slice(c * GLA_CHUNK, (c + 1) * GLA_CHUNK) for c in range(nc)]

    pre = [_dot(z_ref[b, :, 2 * GLA_QW + 2 * GLA_VW:], wg_ref[...]) + bg_ref[...] for b in range(nb)]
    bc, qt, kt, kend = [], [], [], []
    for b in range(nb):
        la = (jnp.minimum(pre[b], 0.0) - jnp.log(1.0 + jnp.exp(-jnp.abs(pre[b])))) * (1.0 / GLA_TAU)
        la_hi, la_lo = _split_bf16(la)
        bc.append(_dot(tri_ref[...], la_hi) + _dot(tri_ref[...], la_lo))
        rem = _dot(upp_ref[...], la_hi) + _dot(upp_ref[...], la_lo)
        q = z_ref[b, :, 0:GLA_QW].astype(F32)
        k = z_ref[b, :, GLA_QW:2 * GLA_QW].astype(F32)
        qt.append((q * (GLA_DK ** -0.5) * jnp.exp(bc[b])).astype(BF16))
        kt.append((k * jnp.exp(-bc[b])).astype(BF16))
        kend.append((k * jnp.exp(rem)).astype(BF16))

    v = [z_ref[b, :, 2 * GLA_QW:2 * GLA_QW + GLA_VW] for b in range(nb)]

    a = {}
    for b in range(nb):
        for c in range(nc):
            krows = jnp.concatenate([kt[b][rows[c]]] * GLA_HEADS, axis=0) * mk_ref[...]
            a[b, c] = _dot_nt(qt[b][rows[c]], krows)
    for b in range(nb):
        for c in range(nc):
            am = jnp.where(causal_ref[...] > 0, a[b, c], 0.0).astype(BF16)
            vbd = jnp.concatenate([v[b][rows[c]]] * GLA_HEADS, axis=0) * mv_ref[...]
            oacc_ref[b, rows[c], :] = _dot(am, vbd)

    def state_increments(c):
        return [_dot(v[b][rows[c]].astype(F32).T.astype(BF16), kend[b][rows[c]]) for b in range(nb)]

    dst = state_increments(0)
    for c in range(nc):
        dst_next = state_increments(c + 1) if c + 1 < nc else None
        for b in range(nb):
            st = st_ref[b]
            oacc_ref[b, rows[c], :] += _dot_nt(qt[b][rows[c]], st.astype(BF16) * mvt_ref[...])
            dec = jnp.exp(bc[b][(c + 1) * GLA_CHUNK - 1:(c + 1) * GLA_CHUNK, :])
            st_ref[b] = st * dec + dst[b]
        dst = dst_next

    for b in range(nb):
        o = oacc_ref[b]
        r = z_ref[b, :, 2 * GLA_QW + GLA_VW:2 * GLA_QW + 2 * GLA_VW].astype(F32)
        ss = _dot((o * o).astype(BF16), bdn_ref[...])
        y = o * lax.rsqrt(ss * (1.0 / GLA_DV) + EPS) * og_ref[...]
        o_ref[b] = (y * (r * jax.nn.sigmoid(r))).astype(BF16)


def _gla(zgla, p, c, tg):
    b, l, _ = zgla.shape
    return pl.pallas_call(
        functools.partial(_gla_kernel, tg=tg),
        grid=(l // tg,),
        in_specs=[
            pl.BlockSpec((b, tg, ZGLA_W), lambda j: (0, j, 0)),
            _const_spec((LANES, GLA_QW)),
            _const_spec((1, GLA_QW)),
            _const_spec((1, GLA_VW)),
            _const_spec((tg, tg)),
            _const_spec((tg, tg)),
            _const_spec((GLA_HEADS * GLA_CHUNK, GLA_QW)),
            _const_spec((GLA_HEADS * GLA_CHUNK, GLA_VW)),
            _const_spec((GLA_VW, GLA_QW)),
            _const_spec((GLA_CHUNK, GLA_HEADS * GLA_CHUNK)),
            _const_spec((GLA_VW, GLA_VW)),
        ],
        out_specs=pl.BlockSpec((b, tg, GLA_VW), lambda j: (0, j, 0)),
        out_shape=jax.ShapeDtypeStruct((b, l, GLA_VW), BF16),
        scratch_shapes=[pltpu.VMEM((b, GLA_VW, GLA_QW), F32), pltpu.VMEM((b, tg, GLA_VW), F32)],
        compiler_params=_cparams(("arbitrary",)),
        name="gla",
    )(zgla, p["w_gate"], p["b_gate"], p["gla_og"], c["gla_tri"], c["gla_upp"], c["gla_mk"], c["gla_mv"],
      c["gla_mvt"], c["gla_causal"], c["gla_bdn"])


def _s5_state_kernel(x_ref, p_ref, dre_ref, dim_ref):
    d = _dot(x_ref[0], p_ref[0])
    dre_ref[...] = d[:, :S5_SW]
    dim_ref[...] = d[:, S5_SW:]


def _s5_state(xb, p8, tr):
    nb, r, _ = xb.shape
    out = jax.ShapeDtypeStruct((r, nb * S5_SW), F32)
    ospec = pl.BlockSpec((tr, S5_SW), lambda i, j: (j, i))
    return pl.pallas_call(
        _s5_state_kernel,
        grid=(nb, r // tr),
        in_specs=[
            pl.BlockSpec((1, tr, S5_BW), lambda i, j: (i, j, 0)),
            pl.BlockSpec((1, S5_BW, 2 * S5_SW), lambda i, j: (i, 0, 0)),
        ],
        out_specs=[ospec, ospec],
        out_shape=[out, out],
        compiler_params=_cparams(("arbitrary", "arbitrary")),
        name="s5_state",
    )(xb, p8)


def _s5_scan_kernel(dre_ref, dim_ref, are_ref, aim_ref, sre_ref, sim_ref, wre_ref, wim_ref, cre_ref, cim_ref,
                    *, cb, nbatch):
    @pl.when(pl.program_id(0) == 0)
    def _():
        wre_ref[...] = jnp.zeros_like(wre_ref)
        wim_ref[...] = jnp.zeros_like(wim_ref)
        cre_ref[...] = jnp.zeros_like(cre_ref)
        cim_ref[...] = jnp.zeros_like(cim_ref)

    nt = wre_ref.shape[0]
    for b in range(nbatch):
        for j in range(nt):
            wre_ref[j, pl.ds(b, cb, stride=SUBLANES), :] = dre_ref[b, :, j * LANES:(j + 1) * LANES]
            wim_ref[j, pl.ds(b, cb, stride=SUBLANES), :] = dim_ref[b, :, j * LANES:(j + 1) * LANES]
    are = jnp.broadcast_to(are_ref[...], cre_ref.shape)
    aim = jnp.broadcast_to(aim_ref[...], cre_ref.shape)

    def body(c, carry):
        s_re, s_im = carry
        rows = pl.ds(pl.multiple_of(c * SUBLANES, SUBLANES), SUBLANES)
        d_re = wre_ref[:, rows, :]
        d_im = wim_ref[:, rows, :]
        wre_ref[:, rows, :] = s_re
        wim_ref[:, rows, :] = s_im
        return are * s_re - aim * s_im + d_re, are * s_im + aim * s_re + d_im

    s_re, s_im = lax.fori_loop(0, cb, body, (cre_ref[...], cim_ref[...]))
    cre_ref[...] = s_re
    cim_ref[...] = s_im
    for b in range(nbatch):
        for j in range(nt):
            sre_ref[b, :, j * LANES:(j + 1) * LANES] = wre_ref[j, pl.ds(b, cb, stride=SUBLANES), :]
            sim_ref[b, :, j * LANES:(j + 1) * LANES] = wim_ref[j, pl.ds(b, cb, stride=SUBLANES), :]


def _s5_scan(dre, dim, are, aim, cb):
    b, c, sw = dre.shape
    nt = sw // LANES
    spec = pl.BlockSpec((b, cb, sw), lambda i: (0, i, 0))
    out = jax.ShapeDtypeStruct((b, c, sw), F32)
    work = pltpu.VMEM((nt, cb * SUBLANES, LANES), F32)
    carry = pltpu.VMEM((nt, SUBLANES, LANES), F32)
    return pl.pallas_call(
        functools.partial(_s5_scan_kernel, cb=cb, nbatch=b),
        grid=(c // cb,),
        in_specs=[spec, spec, _const_spec((nt, 1, LANES)), _const_spec((nt, 1, LANES))],
        out_specs=[spec, spec],
        out_shape=[out, out],
        scratch_shapes=[work, work, carry, carry],
        compiler_params=_cparams(("arbitrary",)),
        name="s5_scan",
    )(dre, dim, are.reshape(nt, 1, LANES), aim.reshape(nt, 1, LANES))


def _gelu_tanh(y):
    return 0.5 * y * (1.0 + jnp.tanh(math.sqrt(2.0 / math.pi) * (y + 0.044715 * (y * y * y))))


def _s5_out_kernel(x_ref, sre_ref, sim_ref, m_ref, q_ref, y_ref):
    s8 = jnp.concatenate([sre_ref[...], sim_ref[...]], axis=1).astype(BF16)
    y = _dot(x_ref[0], m_ref[0]) + _dot(s8, q_ref[0])
    y_ref[0] = _gelu_tanh(y).astype(BF16)


def _s5_out(xb, sre, sim, m8, q8, tr):
    nb, r, _ = xb.shape
    sspec = pl.BlockSpec((tr, S5_SW), lambda i, j: (j, i))
    xspec = pl.BlockSpec((1, tr, S5_BW), lambda i, j: (i, j, 0))
    return pl.pallas_call(
        _s5_out_kernel,
        grid=(nb, r // tr),
        in_specs=[
            xspec, sspec, sspec,
            pl.BlockSpec((1, S5_BW, S5_BW), lambda i, j: (i, 0, 0)),
            pl.BlockSpec((1, 2 * S5_SW, S5_BW), lambda i, j: (i, 0, 0)),
        ],
        out_specs=xspec,
        out_shape=jax.ShapeDtypeStruct((nb, r, S5_BW), BF16),
        compiler_params=_cparams(("arbitrary", "arbitrary")),
        name="s5_out",
    )(xb, sre, sim, m8, q8)


def _s5_params(lam_re, lam_im, b_re, b_im, c_re, c_im, d, log_dt):
    g = S5_GROUPS
    lre = jnp.minimum(lam_re.astype(F32), -1e-4)
    lim = lam_im.astype(F32)
    step = jnp.exp(log_dt.astype(F32))[:, None]
    pw = jnp.arange(S5_T + 1, dtype=F32)[:, None, None]
    mag = jnp.exp(pw * (lre * step)[None])
    ang = pw * (lim * step)[None]
    pre, pim = mag * jnp.cos(ang), mag * jnp.sin(ang)
    nr, ni = pre[1] - 1.0, pim[1]
    den = lre * lre + lim * lim
    cr = (nr * lre + ni * lim) / den
    ci = (ni * lre - nr * lim) / den
    bre = cr[..., None] * b_re - ci[..., None] * b_im
    bim = cr[..., None] * b_im + ci[..., None] * b_re
    lbr = pre[..., None] * bre[None] - pim[..., None] * bim[None]
    lbi = pre[..., None] * bim[None] + pim[..., None] * bre[None]
    kd = (jnp.einsum("gip,dgpj->dgij", c_re, lbr, precision=HIGHEST)
          - jnp.einsum("gip,dgpj->dgij", c_im, lbi, precision=HIGHEST))
    lag = np.arange(S5_T)[None, :] - np.arange(S5_T)[:, None]
    kst = kd[np.clip(lag, 0, S5_T)]
    kst = jnp.where((lag >= 0)[:, :, None, None, None], kst, 0.0)
    m = kst.transpose(2, 0, 4, 1, 3)
    eye_t = jnp.eye(S5_T, dtype=F32)[None, :, None, :, None]
    eye_i = jnp.eye(S5_GROUP, dtype=F32)[None, None, :, None, :]
    m = m + eye_t * eye_i * d.astype(F32)[:, None, None, None, :]
    rev = np.arange(S5_T - 1, -1, -1)
    p_re = lbr[rev].transpose(1, 0, 3, 2)
    p_im = lbi[rev].transpose(1, 0, 3, 2)
    qr = c_re[None] * pre[1:, :, None, :] - c_im[None] * pim[1:, :, None, :]
    qi = -(c_re[None] * pim[1:, :, None, :] + c_im[None] * pre[1:, :, None, :])
    q_re = qr.transpose(1, 3, 0, 2)
    q_im = qi.transpose(1, 3, 0, 2)

    nb, gb = S5_NB, S5_GB
    lanes_ti = np.arange(S5_BW) // LANES * S5_GROUP + np.arange(S5_BW) % S5_GROUP
    exp_ti = jnp.asarray(np.arange(LANES)[:, None] == lanes_ti[None, :], BF16)
    exp_n = jnp.asarray(np.arange(S5_STATE)[:, None] == (np.arange(S5_SW) % S5_STATE)[None, :], BF16)
    g_row_sgj = np.arange(S5_BW) // S5_GROUP % gb
    g_col_tgi = np.arange(S5_BW) % LANES // S5_GROUP
    g_gn = np.arange(S5_SW) // S5_STATE

    def expand(compact, expander, row_g, col_g):
        full = jnp.einsum("brk,kc->brc", compact.astype(BF16), expander, preferred_element_type=BF16)
        return full * jnp.asarray(row_g[:, None] == col_g[None, :], BF16)

    def rows_sgj(a):
        return a.reshape(nb, gb, S5_T, S5_GROUP, -1).transpose(0, 2, 1, 3, 4).reshape(nb, S5_BW, -1)

    m8 = expand(rows_sgj(m.reshape(g, S5_T, S5_GROUP, LANES)), exp_ti, g_row_sgj, g_col_tgi)
    p8 = jnp.concatenate([expand(rows_sgj(p), exp_n, g_row_sgj, g_gn) for p in (p_re, p_im)], axis=2)
    q8 = jnp.concatenate([expand(q.reshape(nb, S5_SW, LANES), exp_ti, g_gn, g_col_tgi) for q in (q_re, q_im)],
                         axis=1)
    are = pre[S5_T].reshape(1, g * S5_STATE)
    aim = pim[S5_T].reshape(1, g * S5_STATE)
    return m8, p8, q8, are, aim


def _s5(xb, sp, b, l):
    m8, p8, q8, are, aim = sp
    c = l // S5_T
    r = b * c
    tr = min(1024, r)
    dre, dim = _s5_state(xb, p8, tr)
    sw = dre.shape[-1]
    sre, sim = _s5_scan(dre.reshape(b, c, sw), dim.reshape(b, c, sw), are, aim, min(64, c))
    return _s5_out(xb, sre.reshape(r, sw), sim.reshape(r, sw), m8, q8, tr)


def _merge_kernel(x_ref, oa_ref, ob_ref, y_ref, g_ref, wa_ref, wb_ref, wc_ref, wglu_ref, bglu_ref, wout_ref,
                  o_ref, ys_ref):
    rows = y_ref.shape[1]
    for b in range(S5_NB):
        for t in range(S5_T):
            ys_ref[b, pl.ds(t, rows, stride=S5_T), :] = y_ref[b, :, t * LANES:(t + 1) * LANES].astype(F32)
    y = jnp.concatenate([ys_ref[b] for b in range(S5_NB)], axis=1)
    oc = (y * jax.nn.sigmoid(_dot(y.astype(BF16), wglu_ref[...]) + bglu_ref[...])).astype(BF16)
    merged = (g_ref[:, 0:D_MODEL].astype(F32) * _dot(oa_ref[...], wa_ref[...])
              + g_ref[:, D_MODEL:2 * D_MODEL].astype(F32) * _dot(ob_ref[...], wb_ref[...])
              + g_ref[:, 2 * D_MODEL:].astype(F32) * _dot(oc, wc_ref[...]))
    o_ref[...] = x_ref[...] + _dot(merged.astype(BF16), wout_ref[...])


def _merge(xf, oa, ob, y, gates, p, tn):
    n = xf.shape[0]
    row = lambda w: pl.BlockSpec((tn, w), lambda i: (i, 0))
    return pl.pallas_call(
        _merge_kernel,
        grid=(n // tn,),
        in_specs=[
            row(D_MODEL), row(BRANCH_W), row(BRANCH_W),
            pl.BlockSpec((S5_NB, tn // S5_T, S5_BW), lambda i: (0, i, 0)), row(GATE_W),
            _const_spec((BRANCH_W, D_MODEL)), _const_spec((BRANCH_W, D_MODEL)), _const_spec((BRANCH_W, D_MODEL)),
            _const_spec((S5_CH, S5_CH)), _const_spec((1, S5_CH)), _const_spec((D_MODEL, D_MODEL)),
        ],
        out_specs=row(D_MODEL),
        out_shape=jax.ShapeDtypeStruct((n, D_MODEL), F32),
        scratch_shapes=[pltpu.VMEM((S5_NB, tn, LANES), F32)],
        compiler_params=_cparams(("arbitrary",)),
        name="merge",
    )(xf, oa, ob, y, gates, p["w_br_mla"], p["w_br_gla"], p["w_br_s5"], p["w_glu"], p["b_glu"], p["w_out"])


def _ffn_kernel(x_ref, g_ref, w1_ref, w2_ref, o_ref, h_ref, *, chunk):
    x = x_ref[...]
    ms = jnp.mean(x * x, axis=-1, keepdims=True)
    h_ref[...] = (x * lax.rsqrt(ms + EPS) * g_ref[...]).astype(BF16)
    o_ref[...] = x
    for c0 in range(0, D_FF, chunk):
        a = jnp.maximum(_dot(h_ref[...], w1_ref[:, c0:c0 + chunk]), 0.0)
        o_ref[...] += _dot((a * a).astype(BF16), w2_ref[c0:c0 + chunk, :])


def _ffn(xf, g, w1, w2, tn):
    n = xf.shape[0]
    return pl.pallas_call(
        functools.partial(_ffn_kernel, chunk=512),
        grid=(n // tn,),
        in_specs=[
            pl.BlockSpec((tn, D_MODEL), lambda i: (i, 0)),
            _const_spec((1, D_MODEL)),
            _const_spec((D_MODEL, D_FF)),
            _const_spec((D_FF, D_MODEL)),
        ],
        out_specs=pl.BlockSpec((tn, D_MODEL), lambda i: (i, 0)),
        out_shape=jax.ShapeDtypeStruct((n, D_MODEL), F32),
        scratch_shapes=[pltpu.VMEM((tn, D_MODEL), BF16)],
        compiler_params=_cparams(("arbitrary",)),
        name="ffn",
    )(xf, g, w1, w2)


def _head_slots(w, width):
    k = w.shape[0]
    w = w.reshape(k, MLA_HEADS, width)
    return jnp.pad(w, ((0, 0), (0, 0), (0, HEAD_SLOT - width))).reshape(k, MLA_HEADS * HEAD_SLOT)


def _swap_rope_halves(a):
    half = MLA_ROPE // 2
    return jnp.concatenate([jnp.zeros_like(a[..., :MLA_NOPE]), a[..., MLA_NOPE + half:], a[..., MLA_NOPE:MLA_NOPE + half]],
                           axis=-1)


def _constants(tg):
    c = {}
    t = np.arange(tg)
    same = (t[:, None] // GLA_CHUNK) == (t[None, :] // GLA_CHUNK)
    c["gla_tri"] = jnp.asarray(same & (t[None, :] <= t[:, None]), BF16)
    c["gla_upp"] = jnp.asarray(same & (t[None, :] > t[:, None]), BF16)
    hs = np.arange(GLA_HEADS * GLA_CHUNK) // GLA_CHUNK
    hk = np.arange(GLA_QW) // GLA_DK
    hv = np.arange(GLA_VW) // GLA_DV
    c["gla_mk"] = jnp.asarray(hs[:, None] == hk[None, :], BF16)
    c["gla_mv"] = jnp.asarray(hs[:, None] == hv[None, :], BF16)
    c["gla_mvt"] = jnp.asarray(hv[:, None] == hk[None, :], BF16)
    s_in = np.arange(GLA_HEADS * GLA_CHUNK) % GLA_CHUNK
    c["gla_causal"] = jnp.asarray(s_in[None, :] <= np.arange(GLA_CHUNK)[:, None], F32)
    c["gla_bdn"] = jnp.asarray(np.kron(np.eye(GLA_HEADS), np.ones((GLA_DV, GLA_DV))), BF16)
    return c


def _rope_tables(l):
    pos = jnp.arange(l, dtype=F32)
    inv_freq = ROPE_BASE ** (-jnp.arange(0, MLA_ROPE, 2, dtype=F32) / MLA_ROPE)
    ang = pos[:, None] * inv_freq[None, :]
    cos, sin = jnp.cos(ang), jnp.sin(ang)
    pad = jnp.zeros((l, LANES - MLA_QK), F32)
    cos128 = jnp.concatenate([jnp.ones((l, MLA_NOPE), F32), cos, cos, pad], axis=-1)
    sin128 = jnp.concatenate([jnp.zeros((l, MLA_NOPE), F32), -sin, sin, pad], axis=-1)
    return cos128, sin128


def _layer_params(lyr, w_in, gate_b, mla_q_norm_g, mla_w_uq, mla_kv_norm_g, mla_w_ukv, mla_q_head_g,
                  mla_k_head_g, gla_w_gate, gla_b_gate, gla_out_g, s5_w_glu, s5_b_glu, w_br_mla, w_br_gla,
                  w_br_s5, w_out):
    p = {}
    w = w_in[lyr]
    sizes = (MLA_Q_RANK, MLA_KV_RANK, MLA_ROPE, GLA_QW, GLA_QW, GLA_VW, GLA_GATE_RANK, GLA_VW, S5_CH, GATE_W)
    offs = np.concatenate([[0], np.cumsum(sizes)])
    cq, ckv, kpe, gq, gk, gv, glr, gr, su, gates = [w[:, offs[i]:offs[i + 1]] for i in range(len(sizes))]
    half = MLA_ROPE // 2
    z = lambda n: jnp.zeros((D_MODEL, n), w.dtype)
    kpe_sw = jnp.concatenate([kpe[:, half:], kpe[:, :half]], axis=1)
    p["w_in"] = jnp.concatenate(
        [cq, ckv, z(MLA_NOPE), kpe, z(LANES - MLA_QK), z(MLA_NOPE), kpe_sw, z(LANES - MLA_QK),
         gq, gk, gv, gr, glr, z(LANES - GLA_GATE_RANK), su, gates], axis=1).astype(BF16)
    p["gate_b"] = gate_b[lyr].reshape(1, GATE_W)

    p["gqn"] = mla_q_norm_g[lyr].reshape(1, MLA_Q_RANK)
    p["gkvn"] = mla_kv_norm_g[lyr].reshape(1, MLA_KV_RANK)
    wuq = mla_w_uq[lyr].reshape(MLA_Q_RANK, MLA_HEADS, MLA_QK)
    p["wq"] = _head_slots(wuq.reshape(MLA_Q_RANK, -1), MLA_QK).astype(BF16)
    p["wqs"] = _head_slots(_swap_rope_halves(wuq).reshape(MLA_Q_RANK, -1), MLA_QK).astype(BF16)
    wukv = mla_w_ukv[lyr].reshape(MLA_KV_RANK, MLA_HEADS, MLA_NOPE + MLA_V)
    p["wk"] = _head_slots(wukv[..., :MLA_NOPE].reshape(MLA_KV_RANK, -1), MLA_NOPE).astype(BF16)
    wv = wukv[..., MLA_NOPE:]
    zv = jnp.zeros_like(wv)
    even = (np.arange(MLA_HEADS) % 2 == 0)[None, :, None]
    p["wv"] = jnp.where(even, jnp.concatenate([wv, zv], -1), jnp.concatenate([zv, wv], -1)).reshape(
        MLA_KV_RANK, MLA_HEADS * HEAD_SLOT).astype(BF16)
    vone = np.zeros((MLA_HEADS, HEAD_SLOT), np.float32)
    vone[0::2, MLA_V] = 1.0
    vone[1::2, 0] = 1.0
    p["vone"] = jnp.asarray(vone.reshape(1, -1))
    pad = lambda g: jnp.pad(g, (0, LANES - MLA_QK)).reshape(1, LANES)
    p["gq"] = pad(mla_q_head_g[lyr])
    p["gqs"] = pad(_swap_rope_halves(mla_q_head_g[lyr]))
    p["gk"] = pad(mla_k_head_g[lyr])
    p["gks"] = pad(_swap_rope_halves(mla_k_head_g[lyr]))

    p["w_gate"] = jnp.pad(gla_w_gate[lyr], ((0, LANES - GLA_GATE_RANK), (0, 0))).astype(BF16)
    p["b_gate"] = gla_b_gate[lyr].reshape(1, GLA_QW)
    p["gla_og"] = jnp.tile(gla_out_g[lyr], GLA_HEADS).reshape(1, GLA_VW)

    p["w_glu"] = s5_w_glu[lyr].astype(BF16)
    p["b_glu"] = s5_b_glu[lyr].reshape(1, S5_CH)
    p["w_br_mla"] = w_br_mla[lyr].astype(BF16)
    p["w_br_gla"] = w_br_gla[lyr].astype(BF16)
    p["w_br_s5"] = w_br_s5[lyr].astype(BF16)
    p["w_out"] = w_out[lyr].astype(BF16)
    return p


def kernel(x, norm1_g, w_in, mla_q_norm_g, mla_w_uq, mla_kv_norm_g, mla_w_ukv, mla_q_head_g, mla_k_head_g,
           gla_w_gate, gla_b_gate, gla_out_g, s5_lam_re, s5_lam_im, s5_b_re, s5_b_im, s5_c_re, s5_c_im, s5_d,
           s5_log_dt, s5_w_glu, s5_b_glu, w_br_mla, w_br_gla, w_br_s5, gate_b, w_out, norm2_g, w_ff1, w_ff2):
    b, l, d = x.shape
    n = b * l
    depth = w_in.shape[0]
    tn = min(TOKEN_TILE, n)
    tl = min(TOKEN_TILE, l)
    tg = min(256, l)
    cos128, sin128 = _rope_tables(l)
    consts = _constants(tg)
    xf = x.reshape(n, d)
    for lyr in range(depth):
        p = _layer_params(lyr, w_in, gate_b, mla_q_norm_g, mla_w_uq, mla_kv_norm_g, mla_w_ukv, mla_q_head_g,
                          mla_k_head_g, gla_w_gate, gla_b_gate, gla_out_g, s5_w_glu, s5_b_glu, w_br_mla,
                          w_br_gla, w_br_s5, w_out)
        sp = _s5_params(s5_lam_re[lyr], s5_lam_im[lyr], s5_b_re[lyr], s5_b_im[lyr], s5_c_re[lyr], s5_c_im[lyr],
                        s5_d[lyr], s5_log_dt[lyr])
        zmla, zgla, su, gates = _in_proj(xf, norm1_g[lyr].reshape(1, d), p["w_in"], p["gate_b"], tn)
        q, k, v = _mla_prep(zmla.reshape(b, l, ZMLA_W), cos128, sin128, p, tl)
        oa = _flash(q, k, v, min(FLASH_TQ, l)).reshape(n, BRANCH_W)
        ob = _gla(zgla.reshape(b, l, ZGLA_W), p, consts, tg).reshape(n, BRANCH_W)
        y = _s5(su, sp, b, l)
        x1 = _merge(xf, oa, ob, y, gates, p, tn)
        xf = _ffn(x1, norm2_g[lyr].reshape(1, d), w_ff1[lyr].astype(BF16), w_ff2[lyr].astype(BF16), tn)
    return xf.reshape(b, l, d)
```

```python
import functools
import math

import numpy as np
import jax
import jax.numpy as jnp
from jax import lax
from jax.experimental import pallas as pl
from jax.experimental.pallas import tpu as pltpu

F32 = jnp.float32
BF16 = jnp.bfloat16
HIGHEST = lax.Precision.HIGHEST

D_MODEL = 1024
MLA_HEADS = 8
MLA_NOPE = 64
MLA_ROPE = 32
MLA_QK = MLA_NOPE + MLA_ROPE
MLA_V = 64
MLA_Q_RANK = 384
MLA_KV_RANK = 256
ROPE_BASE = 10000.0
GLA_HEADS = 4
GLA_DK = 64
GLA_DV = 128
GLA_GATE_RANK = 16
GLA_TAU = 16.0
GLA_CHUNK = 64
S5_CH = 512
S5_GROUP = 16
S5_GROUPS = S5_CH // S5_GROUP
S5_STATE = 64
N_BRANCH = 3
BRANCH_W = 512
D_FF = 4 * D_MODEL
EPS = 1e-6

LANES = 128
HEAD_SLOT = LANES
SUBLANES = 8
S5_T = 8
S5_GB = LANES // S5_GROUP
S5_NB = S5_GROUPS // S5_GB
S5_BW = S5_T * LANES
S5_SW = S5_GB * S5_STATE
VMEM_LIMIT = 56 * 1024 * 1024
NEG = -1e30
FLASH_TQ = 1024
TOKEN_TILE = 1024
FLASH_KEY_SLABS = 4
FLASH_TILES_PER_STEP = 2

ZMLA_W = MLA_Q_RANK + MLA_KV_RANK + 2 * LANES
ZGLA_W = 2 * GLA_HEADS * GLA_DK + 2 * GLA_HEADS * GLA_DV + LANES
GATE_W = N_BRANCH * D_MODEL


def _cparams(sem):
    return pltpu.CompilerParams(dimension_semantics=sem, vmem_limit_bytes=VMEM_LIMIT)


def _const_spec(shape):
    nd = len(shape)
    return pl.BlockSpec(shape, lambda *_: (0,) * nd, pipeline_mode=pl.Buffered(1))


def _dot(a, b):
    return jnp.dot(a, b, preferred_element_type=F32)


def _dot_nt(a, b):
    return lax.dot_general(a, b, (((1,), (1,)), ((), ())), preferred_element_type=F32)


def _in_proj_kernel(x_ref, g_ref, w_ref, gb_ref, zmla_ref, zgla_ref, su_ref, gate_ref, h_ref, s_ref, *, chunk):
    x = x_ref[...]
    ms = jnp.mean(x * x, axis=-1, keepdims=True)
    h_ref[...] = (x * lax.rsqrt(ms + EPS) * g_ref[...]).astype(BF16)
    col = 0
    for out_ref in (zmla_ref, zgla_ref):
        width = out_ref.shape[-1]
        for c0 in range(0, width, chunk):
            c1 = min(c0 + chunk, width)
            out_ref[:, c0:c1] = _dot(h_ref[...], w_ref[:, col + c0:col + c1]).astype(BF16)
        col += width
    su = _dot(h_ref[...], w_ref[:, col:col + S5_CH])
    col += S5_CH
    rows = s_ref.shape[1] // S5_T
    for b in range(S5_NB):
        s_ref[b] = su[:, b * LANES:(b + 1) * LANES]
        for t in range(S5_T):
            su_ref[b, :, t * LANES:(t + 1) * LANES] = s_ref[b, pl.ds(t, rows, stride=S5_T), :].astype(BF16)
    for c0 in range(0, GATE_W, chunk):
        pre = _dot(h_ref[...], w_ref[:, col + c0:col + c0 + chunk]) + gb_ref[:, c0:c0 + chunk]
        gate_ref[:, c0:c0 + chunk] = jax.nn.sigmoid(pre).astype(BF16)


def _in_proj(xf, g, w, gate_b, tn):
    n = xf.shape[0]
    wtot = w.shape[1]
    return pl.pallas_call(
        functools.partial(_in_proj_kernel, chunk=512),
        grid=(n // tn,),
        in_specs=[
            pl.BlockSpec((tn, D_MODEL), lambda i: (i, 0)),
            _const_spec((1, D_MODEL)),
            _const_spec((D_MODEL, wtot)),
            _const_spec((1, GATE_W)),
        ],
        out_specs=[
            pl.BlockSpec((tn, ZMLA_W), lambda i: (i, 0)),
            pl.BlockSpec((tn, ZGLA_W), lambda i: (i, 0)),
            pl.BlockSpec((S5_NB, tn // S5_T, S5_BW), lambda i: (0, i, 0)),
            pl.BlockSpec((tn, GATE_W), lambda i: (i, 0)),
        ],
        out_shape=[
            jax.ShapeDtypeStruct((n, ZMLA_W), BF16),
            jax.ShapeDtypeStruct((n, ZGLA_W), BF16),
            jax.ShapeDtypeStruct((S5_NB, n // S5_T, S5_BW), BF16),
            jax.ShapeDtypeStruct((n, GATE_W), BF16),
        ],
        scratch_shapes=[pltpu.VMEM((tn, D_MODEL), BF16), pltpu.VMEM((S5_NB, tn, LANES), F32)],
        compiler_params=_cparams(("arbitrary",)),
        name="in_proj",
    )(xf, g, w, gate_b)


def _mla_prep_kernel(z_ref, cos_ref, sin_ref, gqn_ref, gkvn_ref, wq_ref, wqs_ref, wk_ref, wv_ref,
                     gq_ref, gqs_ref, gk_ref, gks_ref, vone_ref, q_ref, k_ref, v_ref):
    cq = z_ref[0, :, 0:MLA_Q_RANK].astype(F32)
    ckv = z_ref[0, :, MLA_Q_RANK:MLA_Q_RANK + MLA_KV_RANK].astype(F32)
    o = MLA_Q_RANK + MLA_KV_RANK
    kpe = z_ref[0, :, o:o + LANES].astype(F32)
    kpe_sw = z_ref[0, :, o + LANES:o + 2 * LANES].astype(F32)
    cos = cos_ref[...]
    sin = sin_ref[...]

    cqn = (cq * lax.rsqrt(jnp.mean(cq * cq, axis=-1, keepdims=True) + EPS) * gqn_ref[...]).astype(BF16)
    ckvn = (ckv * lax.rsqrt(jnp.mean(ckv * ckv, axis=-1, keepdims=True) + EPS) * gkvn_ref[...]).astype(BF16)

    q_raw = _dot(cqn, wq_ref[...])
    q_sw = _dot(cqn, wqs_ref[...])
    k_nope = _dot(ckvn, wk_ref[...])
    v_all = _dot(ckvn, wv_ref[...]) + vone_ref[...]

    ssq_pe = jnp.sum(kpe * kpe, axis=-1, keepdims=True)
    q_scale = MLA_QK ** -0.5 * math.log2(math.e)

    cq_t = gq_ref[...] * cos
    sq_t = gqs_ref[...] * sin
    ck_t = gk_ref[...] * cos
    kpe_rot = kpe * ck_t + kpe_sw * (gks_ref[...] * sin)
    for h in range(MLA_HEADS):
        sl = slice(h * HEAD_SLOT, (h + 1) * HEAD_SLOT)
        qr, kn = q_raw[:, sl], k_nope[:, sl]
        rq = lax.rsqrt(jnp.sum(qr * qr, axis=-1, keepdims=True) * (1.0 / MLA_QK) + EPS) * q_scale
        qh = rq * (qr * cq_t + q_sw[:, sl] * sq_t)
        q_ref[0, h] = qh.astype(BF16)
        rk = lax.rsqrt((jnp.sum(kn * kn, axis=-1, keepdims=True) + ssq_pe) * (1.0 / MLA_QK) + EPS)
        kh = rk * (kn * ck_t + kpe_rot)
        k_ref[0, h] = kh.astype(BF16)
        v_ref[0, h] = v_all[:, sl].T.astype(BF16)


def _mla_prep(zmla, cos128, sin128, p, tl):
    b, l, _ = zmla.shape
    hw = MLA_HEADS * HEAD_SLOT
    head_out = jax.ShapeDtypeStruct((b, MLA_HEADS, l, HEAD_SLOT), BF16)
    head_spec = pl.BlockSpec((1, MLA_HEADS, tl, HEAD_SLOT), lambda i, j: (i, 0, j, 0))
    return pl.pallas_call(
        _mla_prep_kernel,
        grid=(b, l // tl),
        in_specs=[
            pl.BlockSpec((1, tl, ZMLA_W), lambda i, j: (i, j, 0)),
            pl.BlockSpec((tl, LANES), lambda i, j: (j, 0)),
            pl.BlockSpec((tl, LANES), lambda i, j: (j, 0)),
            _const_spec((1, MLA_Q_RANK)),
            _const_spec((1, MLA_KV_RANK)),
            _const_spec((MLA_Q_RANK, hw)),
            _const_spec((MLA_Q_RANK, hw)),
            _const_spec((MLA_KV_RANK, hw)),
            _const_spec((MLA_KV_RANK, hw)),
            _const_spec((1, LANES)),
            _const_spec((1, LANES)),
            _const_spec((1, LANES)),
            _const_spec((1, LANES)),
            _const_spec((1, hw)),
        ],
        out_specs=[head_spec, head_spec,
                   pl.BlockSpec((1, MLA_HEADS, HEAD_SLOT, tl), lambda i, j: (i, 0, 0, j))],
        out_shape=[head_out, head_out, jax.ShapeDtypeStruct((b, MLA_HEADS, HEAD_SLOT, l), BF16)],
        compiler_params=_cparams(("arbitrary", "arbitrary")),
        name="mla_prep",
    )(zmla, cos128, sin128, p["gqn"], p["gkvn"], p["wq"], p["wqs"], p["wk"], p["wv"],
      p["gq"], p["gqs"], p["gk"], p["gks"], p["vone"])


def _flash_kernel(q_ref, k_ref, vt_ref, o_ref, *, tq):
    tiles = q_ref.shape[2] // tq
    for t in range(tiles):
        rows = slice(t * tq, (t + 1) * tq)
        _flash_tile(pl.program_id(2) * tiles + t, q_ref.at[:, :, rows, :], k_ref, vt_ref,
                    o_ref.at[:, rows, :], tq)


def _flash_tile(qi, q_ref, k_ref, vt_ref, o_ref, tq):
    srow = lax.broadcasted_iota(jnp.int32, (HEAD_SLOT, tq), 0)

    def _causal(nk, nq, q_off):
        return (lax.broadcasted_iota(jnp.int32, (nk, nq), 0)
                <= lax.broadcasted_iota(jnp.int32, (nk, nq), 1) + q_off)

    def update(m, acc, s, vt):
        sb = s.astype(BF16)
        m_new = jnp.maximum(m, jnp.max(sb, axis=0, keepdims=True).astype(F32))
        alpha = jnp.exp2(m - m_new)
        p = jnp.exp2(sb - m_new.astype(BF16))
        return m_new, alpha * acc + _dot(vt, p)

    half = tq // 2

    kw = tq // FLASH_KEY_SLABS

    def step(j, carry):
        carry = [list(c) for c in carry]

        def slab_scores(hh, ks):
            start = pl.multiple_of(j * tq + ks * kw, kw)
            return _dot_nt(k_ref[0, hh, pl.ds(start, kw), :], q_ref[0, hh])

        pending = [slab_scores(hh, 0) for hh in range(2)]
        for ks in range(FLASH_KEY_SLABS):
            start = pl.multiple_of(j * tq + ks * kw, kw)
            for hh in range(2):
                s = pending[hh]
                if ks + 1 < FLASH_KEY_SLABS:
                    pending[hh] = slab_scores(hh, ks + 1)
                vt = vt_ref[0, hh, :, pl.ds(start, kw)]
                carry[hh] = [update(*carry[hh][0], s[:, :half], vt), update(*carry[hh][1], s[:, half:], vt)]
        return tuple(tuple(c) for c in carry)

    def diagonal(carry):
        start = pl.multiple_of(qi * tq, tq)
        s_lo = [_dot_nt(k_ref[0, hh, pl.ds(start, half), :], q_ref[0, hh, 0:half, :]) for hh in range(2)]
        s_hi = [_dot_nt(k_ref[0, hh, pl.ds(start, tq), :], q_ref[0, hh, half:, :]) for hh in range(2)]
        accs = []
        for hh in range(2):
            lo = jnp.where(_causal(half, half, 0), s_lo[hh], NEG)
            hi = jnp.where(_causal(tq, half, half), s_hi[hh], NEG)
            _, a_lo = update(*carry[hh][0], lo, vt_ref[0, hh, :, pl.ds(start, half)])
            _, a_hi = update(*carry[hh][1], hi, vt_ref[0, hh, :, pl.ds(start, tq)])
            accs.append(jnp.concatenate([a_lo, a_hi], axis=1))
        return accs

    init = (jnp.full((1, half), NEG, F32), jnp.zeros((HEAD_SLOT, half), F32))
    carry = lax.fori_loop(0, qi, step, ((init, init), (init, init)))
    acc0, acc1 = diagonal(carry)
    o0 = acc0 / acc0[MLA_V:MLA_V + 1, :]
    o1 = acc1 / acc1[0:1, :]
    o_ref[0] = jnp.where(srow < MLA_V, o0, o1).T.astype(BF16)


def _flash(q, k, vt, tq):
    b, h, l, _ = q.shape
    ts = tq * FLASH_TILES_PER_STEP if l % (tq * FLASH_TILES_PER_STEP) == 0 else tq
    return pl.pallas_call(
        functools.partial(_flash_kernel, tq=tq),
        grid=(b, h // 2, l // ts),
        in_specs=[pl.BlockSpec((1, 2, ts, HEAD_SLOT), lambda i, j, t: (i, j, t, 0)),
                  pl.BlockSpec((1, 2, l, HEAD_SLOT), lambda i, j, t: (i, j, 0, 0)),
                  pl.BlockSpec((1, 2, HEAD_SLOT, l), lambda i, j, t: (i, j, 0, 0))],
        out_specs=pl.BlockSpec((1, ts, LANES), lambda i, j, t: (i, t, j)),
        out_shape=jax.ShapeDtypeStruct((b, l, h * MLA_V), BF16),
        compiler_params=_cparams(("arbitrary", "arbitrary", "arbitrary")),
        name="flash",
    )(q, k, vt)


GLA_QW = GLA_HEADS * GLA_DK
GLA_VW = GLA_HEADS * GLA_DV


def _split_bf16(a):
    hi = a.astype(BF16)
    lo = (a - hi.astype(F32)).astype(BF16)
    return hi, lo


def _gla_kernel(z_ref, wg_ref, bg_ref, og_ref, tri_ref, upp_ref, mk_ref, mv_ref, mvt_ref, causal_ref, bdn_ref,
                o_ref, st_ref, oacc_ref, *, tg):
    @pl.when(pl.program_id(0) == 0)
    def _():
        st_ref[...] = jnp.zeros_like(st_ref)

    nb = z_ref.shape[0]
    nc = tg // GLA_CHUNK
    rows = [slice(c * GLA_CHUNK, (c + 1) * GLA_CHUNK) for c in range(nc)]

    pre = [_dot(z_ref[b, :, 2 * GLA_QW + 2 * GLA_VW:], wg_ref[...]) + bg_ref[...] for b in range(nb)]
    bc, qt, kt, kend = [], [], [], []
    for b in range(nb):
        la = (jnp.minimum(pre[b], 0.0) - jnp.log(1.0 + jnp.exp(-jnp.abs(pre[b])))) * (1.0 / GLA_TAU)
        la_hi, la_lo = _split_bf16(la)
        bc.append(_dot(tri_ref[...], la_hi) + _dot(tri_ref[...], la_lo))
        rem = _dot(upp_ref[...], la_hi) + _dot(upp_ref[...], la_lo)
        q = z_ref[b, :, 0:GLA_QW].astype(F32)
        k = z_ref[b, :, GLA_QW:2 * GLA_QW].astype(F32)
        qt.append((q * (GLA_DK ** -0.5) * jnp.exp(bc[b])).astype(BF16))
        kt.append((k * jnp.exp(-bc[b])).astype(BF16))
        kend.append((k * jnp.exp(rem)).astype(BF16))

    v = [z_ref[b, :, 2 * GLA_QW:2 * GLA_QW + GLA_VW] for b in range(nb)]

    a = {}
    for b in range(nb):
        for c in range(nc):
            krows = jnp.concatenate([kt[b][rows[c]]] * GLA_HEADS, axis=0) * mk_ref[...]
            a[b, c] = _dot_nt(qt[b][rows[c]], krows)
    for b in range(nb):
        for c in range(nc):
            am = jnp.where(causal_ref[...] > 0, a[b, c], 0.0).astype(BF16)
            vbd = jnp.concatenate([v[b][rows[c]]] * GLA_HEADS, axis=0) * mv_ref[...]
            oacc_ref[b, rows[c], :] = _dot(am, vbd)

    def state_increments(c):
        return [_dot(v[b][rows[c]].astype(F32).T.astype(BF16), kend[b][rows[c]]) for b in range(nb)]

    dst = state_increments(0)
    for c in range(nc):
        dst_next = state_increments(c + 1) if c + 1 < nc else None
        for b in range(nb):
            st = st_ref[b]
            oacc_ref[b, rows[c], :] += _dot_nt(qt[b][rows[c]], st.astype(BF16) * mvt_ref[...])
            dec = jnp.exp(bc[b][(c + 1) * GLA_CHUNK - 1:(c + 1) * GLA_CHUNK, :])
            st_ref[b] = st * dec + dst[b]
        dst = dst_next

    for b in range(nb):
        o = oacc_ref[b]
        r = z_ref[b, :, 2 * GLA_QW + GLA_VW:2 * GLA_QW + 2 * GLA_VW].astype(F32)
        ss = _dot((o * o).astype(BF16), bdn_ref[...])
        y = o * lax.rsqrt(ss * (1.0 / GLA_DV) + EPS) * og_ref[...]
        o_ref[b] = (y * (r * jax.nn.sigmoid(r))).astype(BF16)


def _gla(zgla, p, c, tg):
    b, l, _ = zgla.shape
    return pl.pallas_call(
        functools.partial(_gla_kernel, tg=tg),
        grid=(l // tg,),
        in_specs=[
            pl.BlockSpec((b, tg, ZGLA_W), lambda j: (0, j, 0)),
            _const_spec((LANES, GLA_QW)),
            _const_spec((1, GLA_QW)),
            _const_spec((1, GLA_VW)),
            _const_spec((tg, tg)),
            _const_spec((tg, tg)),
            _const_spec((GLA_HEADS * GLA_CHUNK, GLA_QW)),
            _const_spec((GLA_HEADS * GLA_CHUNK, GLA_VW)),
            _const_spec((GLA_VW, GLA_QW)),
            _const_spec((GLA_CHUNK, GLA_HEADS * GLA_CHUNK)),
            _const_spec((GLA_VW, GLA_VW)),
        ],
        out_specs=pl.BlockSpec((b, tg, GLA_VW), lambda j: (0, j, 0)),
        out_shape=jax.ShapeDtypeStruct((b, l, GLA_VW), BF16),
        scratch_shapes=[pltpu.VMEM((b, GLA_VW, GLA_QW), F32), pltpu.VMEM((b, tg, GLA_VW), F32)],
        compiler_params=_cparams(("arbitrary",)),
        name="gla",
    )(zgla, p["w_gate"], p["b_gate"], p["gla_og"], c["gla_tri"], c["gla_upp"], c["gla_mk"], c["gla_mv"],
      c["gla_mvt"], c["gla_causal"], c["gla_bdn"])


def _s5_state_kernel(x_ref, p_ref, dre_ref, dim_ref):
    d = _dot(x_ref[0], p_ref[0])
    dre_ref[...] = d[:, :S5_SW]
    dim_ref[...] = d[:, S5_SW:]


def _s5_state(xb, p8, tr):
    nb, r, _ = xb.shape
    out = jax.ShapeDtypeStruct((r, nb * S5_SW), F32)
    ospec = pl.BlockSpec((tr, S5_SW), lambda i, j: (j, i))
    return pl.pallas_call(
        _s5_state_kernel,
        grid=(nb, r // tr),
        in_specs=[
            pl.BlockSpec((1, tr, S5_BW), lambda i, j: (i, j, 0)),
            pl.BlockSpec((1, S5_BW, 2 * S5_SW), lambda i, j: (i, 0, 0)),
        ],
        out_specs=[ospec, ospec],
        out_shape=[out, out],
        compiler_params=_cparams(("arbitrary", "arbitrary")),
        name="s5_state",
    )(xb, p8)


def _s5_scan_kernel(dre_ref, dim_ref, are_ref, aim_ref, sre_ref, sim_ref, wre_ref, wim_ref, cre_ref, cim_ref,
                    *, cb, nbatch):
    @pl.when(pl.program_id(0) == 0)
    def _():
        wre_ref[...] = jnp.zeros_like(wre_ref)
        wim_ref[...] = jnp.zeros_like(wim_ref)
        cre_ref[...] = jnp.zeros_like(cre_ref)
        cim_ref[...] = jnp.zeros_like(cim_ref)

    nt = wre_ref.shape[0]
    for b in range(nbatch):
        for j in range(nt):
            wre_ref[j, pl.ds(b, cb, stride=SUBLANES), :] = dre_ref[b, :, j * LANES:(j + 1) * LANES]
            wim_ref[j, pl.ds(b, cb, stride=SUBLANES), :] = dim_ref[b, :, j * LANES:(j + 1) * LANES]
    are = jnp.broadcast_to(are_ref[...], cre_ref.shape)
    aim = jnp.broadcast_to(aim_ref[...], cre_ref.shape)

    def body(c, carry):
        s_re, s_im = carry
        rows = pl.ds(pl.multiple_of(c * SUBLANES, SUBLANES), SUBLANES)
        d_re = wre_ref[:, rows, :]
        d_im = wim_ref[:, rows, :]
        wre_ref[:, rows, :] = s_re
        wim_ref[:, rows, :] = s_im
        return are * s_re - aim * s_im + d_re, are * s_im + aim * s_re + d_im

    s_re, s_im = lax.fori_loop(0, cb, body, (cre_ref[...], cim_ref[...]))
    cre_ref[...] = s_re
    cim_ref[...] = s_im
    for b in range(nbatch):
        for j in range(nt):
            sre_ref[b, :, j * LANES:(j + 1) * LANES] = wre_ref[j, pl.ds(b, cb, stride=SUBLANES), :]
            sim_ref[b, :, j * LANES:(j + 1) * LANES] = wim_ref[j, pl.ds(b, cb, stride=SUBLANES), :]


def _s5_scan(dre, dim, are, aim, cb):
    b, c, sw = dre.shape
    nt = sw // LANES
    spec = pl.BlockSpec((b, cb, sw), lambda i: (0, i, 0))
    out = jax.ShapeDtypeStruct((b, c, sw), F32)
    work = pltpu.VMEM((nt, cb * SUBLANES, LANES), F32)
    carry = pltpu.VMEM((nt, SUBLANES, LANES), F32)
    return pl.pallas_call(
        functools.partial(_s5_scan_kernel, cb=cb, nbatch=b),
        grid=(c // cb,),
        in_specs=[spec, spec, _const_spec((nt, 1, LANES)), _const_spec((nt, 1, LANES))],
        out_specs=[spec, spec],
        out_shape=[out, out],
        scratch_shapes=[work, work, carry, carry],
        compiler_params=_cparams(("arbitrary",)),
        name="s5_scan",
    )(dre, dim, are.reshape(nt, 1, LANES), aim.reshape(nt, 1, LANES))


def _gelu_tanh(y):
    return 0.5 * y * (1.0 + jnp.tanh(math.sqrt(2.0 / math.pi) * (y + 0.044715 * (y * y * y))))


def _s5_out_kernel(x_ref, sre_ref, sim_ref, m_ref, q_ref, y_ref):
    s8 = jnp.concatenate([sre_ref[...], sim_ref[...]], axis=1).astype(BF16)
    y = _dot(x_ref[0], m_ref[0]) + _dot(s8, q_ref[0])
    y_ref[0] = _gelu_tanh(y).astype(BF16)


def _s5_out(xb, sre, sim, m8, q8, tr):
    nb, r, _ = xb.shape
    sspec = pl.BlockSpec((tr, S5_SW), lambda i, j: (j, i))
    xspec = pl.BlockSpec((1, tr, S5_BW), lambda i, j: (i, j, 0))
    return pl.pallas_call(
        _s5_out_kernel,
        grid=(nb, r // tr),
        in_specs=[
            xspec, sspec, sspec,
            pl.BlockSpec((1, S5_BW, S5_BW), lambda i, j: (i, 0, 0)),
            pl.BlockSpec((1, 2 * S5_SW, S5_BW), lambda i, j: (i, 0, 0)),
        ],
        out_specs=xspec,
        out_shape=jax.ShapeDtypeStruct((nb, r, S5_BW), BF16),
        compiler_params=_cparams(("arbitrary", "arbitrary")),
        name="s5_out",
    )(xb, sre, sim, m8, q8)


def _s5_params(lam_re, lam_im, b_re, b_im, c_re, c_im, d, log_dt):
    g = S5_GROUPS
    lre = jnp.minimum(lam_re.astype(F32), -1e-4)
    lim = lam_im.astype(F32)
    step = jnp.exp(log_dt.astype(F32))[:, None]
    pw = jnp.arange(S5_T + 1, dtype=F32)[:, None, None]
    mag = jnp.exp(pw * (lre * step)[None])
    ang = pw * (lim * step)[None]
    pre, pim = mag * jnp.cos(ang), mag * jnp.sin(ang)
    nr, ni = pre[1] - 1.0, pim[1]
    den = lre * lre + lim * lim
    cr = (nr * lre + ni * lim) / den
    ci = (ni * lre - nr * lim) / den
    bre = cr[..., None] * b_re - ci[..., None] * b_im
    bim = cr[..., None] * b_im + ci[..., None] * b_re
    lbr = pre[..., None] * bre[None] - pim[..., None] * bim[None]
    lbi = pre[..., None] * bim[None] + pim[..., None] * bre[None]
    kd = (jnp.einsum("gip,dgpj->dgij", c_re, lbr, precision=HIGHEST)
          - jnp.einsum("gip,dgpj->dgij", c_im, lbi, precision=HIGHEST))
    lag = np.arange(S5_T)[None, :] - np.arange(S5_T)[:, None]
    kst = kd[np.clip(lag, 0, S5_T)]
    kst = jnp.where((lag >= 0)[:, :, None, None, None], kst, 0.0)
    m = kst.transpose(2, 0, 4, 1, 3)
    eye_t = jnp.eye(S5_T, dtype=F32)[None, :, None, :, None]
    eye_i = jnp.eye(S5_GROUP, dtype=F32)[None, None, :, None, :]
    m = m + eye_t * eye_i * d.astype(F32)[:, None, None, None, :]
    rev = np.arange(S5_T - 1, -1, -1)
    p_re = lbr[rev].transpose(1, 0, 3, 2)
    p_im = lbi[rev].transpose(1, 0, 3, 2)
    qr = c_re[None] * pre[1:, :, None, :] - c_im[None] * pim[1:, :, None, :]
    qi = -(c_re[None] * pim[1:, :, None, :] + c_im[None] * pre[1:, :, None, :])
    q_re = qr.transpose(1, 3, 0, 2)
    q_im = qi.transpose(1, 3, 0, 2)

    nb, gb = S5_NB, S5_GB
    lanes_ti = np.arange(S5_BW) // LANES * S5_GROUP + np.arange(S5_BW) % S5_GROUP
    exp_ti = jnp.asarray(np.arange(LANES)[:, None] == lanes_ti[None, :], BF16)
    exp_n = jnp.asarray(np.arange(S5_STATE)[:, None] == (np.arange(S5_SW) % S5_STATE)[None, :], BF16)
    g_row_sgj = np.arange(S5_BW) // S5_GROUP % gb
    g_col_tgi = np.arange(S5_BW) % LANES // S5_GROUP
    g_gn = np.arange(S5_SW) // S5_STATE

    def expand(compact, expander, row_g, col_g):
        full = jnp.einsum("brk,kc->brc", compact.astype(BF16), expander, preferred_element_type=BF16)
        return full * jnp.asarray(row_g[:, None] == col_g[None, :], BF16)

    def rows_sgj(a):
        return a.reshape(nb, gb, S5_T, S5_GROUP, -1).transpose(0, 2, 1, 3, 4).reshape(nb, S5_BW, -1)

    m8 = expand(rows_sgj(m.reshape(g, S5_T, S5_GROUP, LANES)), exp_ti, g_row_sgj, g_col_tgi)
    p8 = jnp.concatenate([expand(rows_sgj(p), exp_n, g_row_sgj, g_gn) for p in (p_re, p_im)], axis=2)
    q8 = jnp.concatenate([expand(q.reshape(nb, S5_SW, LANES), exp_ti, g_gn, g_col_tgi) for q in (q_re, q_im)],
                         axis=1)
    are = pre[S5_T].reshape(1, g * S5_STATE)
    aim = pim[S5_T].reshape(1, g * S5_STATE)
    return m8, p8, q8, are, aim


def _s5(xb, sp, b, l):
    m8, p8, q8, are, aim = sp
    c = l // S5_T
    r = b * c
    tr = min(1024, r)
    dre, dim = _s5_state(xb, p8, tr)
    sw = dre.shape[-1]
    sre, sim = _s5_scan(dre.reshape(b, c, sw), dim.reshape(b, c, sw), are, aim, min(64, c))
    return _s5_out(xb, sre.reshape(r, sw), sim.reshape(r, sw), m8, q8, tr)


def _merge_kernel(x_ref, oa_ref, ob_ref, y_ref, g_ref, wa_ref, wb_ref, wc_ref, wglu_ref, bglu_ref, wout_ref,
                  o_ref, ys_ref):
    rows = y_ref.shape[1]
    for b in range(S5_NB):
        for t in range(S5_T):
            ys_ref[b, pl.ds(t, rows, stride=S5_T), :] = y_ref[b, :, t * LANES:(t + 1) * LANES].astype(F32)
    y = jnp.concatenate([ys_ref[b] for b in range(S5_NB)], axis=1)
    oc = (y * jax.nn.sigmoid(_dot(y.astype(BF16), wglu_ref[...]) + bglu_ref[...])).astype(BF16)
    merged = (g_ref[:, 0:D_MODEL].astype(F32) * _dot(oa_ref[...], wa_ref[...])
              + g_ref[:, D_MODEL:2 * D_MODEL].astype(F32) * _dot(ob_ref[...], wb_ref[...])
              + g_ref[:, 2 * D_MODEL:].astype(F32) * _dot(oc, wc_ref[...]))
    o_ref[...] = x_ref[...] + _dot(merged.astype(BF16), wout_ref[...])


def _merge(xf, oa, ob, y, gates, p, tn):
    n = xf.shape[0]
    row = lambda w: pl.BlockSpec((tn, w), lambda i: (i, 0))
    return pl.pallas_call(
        _merge_kernel,
        grid=(n // tn,),
        in_specs=[
            row(D_MODEL), row(BRANCH_W), row(BRANCH_W),
            pl.BlockSpec((S5_NB, tn // S5_T, S5_BW), lambda i: (0, i, 0)), row(GATE_W),
            _const_spec((BRANCH_W, D_MODEL)), _const_spec((BRANCH_W, D_MODEL)), _const_spec((BRANCH_W, D_MODEL)),
            _const_spec((S5_CH, S5_CH)), _const_spec((1, S5_CH)), _const_spec((D_MODEL, D_MODEL)),
        ],
        out_specs=row(D_MODEL),
        out_shape=jax.ShapeDtypeStruct((n, D_MODEL), F32),
        scratch_shapes=[pltpu.VMEM((S5_NB, tn, LANES), F32)],
        compiler_params=_cparams(("arbitrary",)),
        name="merge",
    )(xf, oa, ob, y, gates, p["w_br_mla"], p["w_br_gla"], p["w_br_s5"], p["w_glu"], p["b_glu"], p["w_out"])


def _ffn_kernel(x_ref, g_ref, w1_ref, w2_ref, o_ref, h_ref, *, chunk):
    x = x_ref[...]
    ms = jnp.mean(x * x, axis=-1, keepdims=True)
    h_ref[...] = (x * lax.rsqrt(ms + EPS) * g_ref[...]).astype(BF16)
    o_ref[...] = x
    for c0 in range(0, D_FF, chunk):
        a = jnp.maximum(_dot(h_ref[...], w1_ref[:, c0:c0 + chunk]), 0.0)
        o_ref[...] += _dot((a * a).astype(BF16), w2_ref[c0:c0 + chunk, :])


def _ffn(xf, g, w1, w2, tn):
    n = xf.shape[0]
    return pl.pallas_call(
        functools.partial(_ffn_kernel, chunk=512),
        grid=(n // tn,),
        in_specs=[
            pl.BlockSpec((tn, D_MODEL), lambda i: (i, 0)),
            _const_spec((1, D_MODEL)),
            _const_spec((D_MODEL, D_FF)),
            _const_spec((D_FF, D_MODEL)),
        ],
        out_specs=pl.BlockSpec((tn, D_MODEL), lambda i: (i, 0)),
        out_shape=jax.ShapeDtypeStruct((n, D_MODEL), F32),
        scratch_shapes=[pltpu.VMEM((tn, D_MODEL), BF16)],
        compiler_params=_cparams(("arbitrary",)),
        name="ffn",
    )(xf, g, w1, w2)


def _head_slots(w, width):
    k = w.shape[0]
    w = w.reshape(k, MLA_HEADS, width)
    return jnp.pad(w, ((0, 0), (0, 0), (0, HEAD_SLOT - width))).reshape(k, MLA_HEADS * HEAD_SLOT)


def _swap_rope_halves(a):
    half = MLA_ROPE // 2
    return jnp.concatenate([jnp.zeros_like(a[..., :MLA_NOPE]), a[..., MLA_NOPE + half:], a[..., MLA_NOPE:MLA_NOPE + half]],
                           axis=-1)


def _constants(tg):
    c = {}
    t = np.arange(tg)
    same = (t[:, None] // GLA_CHUNK) == (t[None, :] // GLA_CHUNK)
    c["gla_tri"] = jnp.asarray(same & (t[None, :] <= t[:, None]), BF16)
    c["gla_upp"] = jnp.asarray(same & (t[None, :] > t[:, None]), BF16)
    hs = np.arange(GLA_HEADS * GLA_CHUNK) // GLA_CHUNK
    hk = np.arange(GLA_QW) // GLA_DK
    hv = np.arange(GLA_VW) // GLA_DV
    c["gla_mk"] = jnp.asarray(hs[:, None] == hk[None, :], BF16)
    c["gla_mv"] = jnp.asarray(hs[:, None] == hv[None, :], BF16)
    c["gla_mvt"] = jnp.asarray(hv[:, None] == hk[None, :], BF16)
    s_in = np.arange(GLA_HEADS * GLA_CHUNK) % GLA_CHUNK
    c["gla_causal"] = jnp.asarray(s_in[None, :] <= np.arange(GLA_CHUNK)[:, None], F32)
    c["gla_bdn"] = jnp.asarray(np.kron(np.eye(GLA_HEADS), np.ones((GLA_DV, GLA_DV))), BF16)
    return c


def _rope_tables(l):
    pos = jnp.arange(l, dtype=F32)
    inv_freq = ROPE_BASE ** (-jnp.arange(0, MLA_ROPE, 2, dtype=F32) / MLA_ROPE)
    ang = pos[:, None] * inv_freq[None, :]
    cos, sin = lax.optimization_barrier((jnp.cos(ang), jnp.sin(ang)))
    pad = jnp.zeros((l, LANES - MLA_QK), F32)
    cos128 = jnp.concatenate([jnp.ones((l, MLA_NOPE), F32), cos, cos, pad], axis=-1)
    sin128 = jnp.concatenate([jnp.zeros((l, MLA_NOPE), F32), -sin, sin, pad], axis=-1)
    return cos128, sin128


def _layer_params(lyr, w_in, gate_b, mla_q_norm_g, mla_w_uq, mla_kv_norm_g, mla_w_ukv, mla_q_head_g,
                  mla_k_head_g, gla_w_gate, gla_b_gate, gla_out_g, s5_w_glu, s5_b_glu, w_br_mla, w_br_gla,
                  w_br_s5, w_out):
    p = {}
    w = w_in[lyr]
    sizes = (MLA_Q_RANK, MLA_KV_RANK, MLA_ROPE, GLA_QW, GLA_QW, GLA_VW, GLA_GATE_RANK, GLA_VW, S5_CH, GATE_W)
    offs = np.concatenate([[0], np.cumsum(sizes)])
    cq, ckv, kpe, gq, gk, gv, glr, gr, su, gates = [w[:, offs[i]:offs[i + 1]] for i in range(len(sizes))]
    half = MLA_ROPE // 2
    z = lambda n: jnp.zeros((D_MODEL, n), w.dtype)
    kpe_sw = jnp.concatenate([kpe[:, half:], kpe[:, :half]], axis=1)
    p["w_in"] = jnp.concatenate(
        [cq, ckv, z(MLA_NOPE), kpe, z(LANES - MLA_QK), z(MLA_NOPE), kpe_sw, z(LANES - MLA_QK),
         gq, gk, gv, gr, glr, z(LANES - GLA_GATE_RANK), su, gates], axis=1).astype(BF16)
    p["gate_b"] = gate_b[lyr].reshape(1, GATE_W)

    p["gqn"] = mla_q_norm_g[lyr].reshape(1, MLA_Q_RANK)
    p["gkvn"] = mla_kv_norm_g[lyr].reshape(1, MLA_KV_RANK)
    wuq = mla_w_uq[lyr].reshape(MLA_Q_RANK, MLA_HEADS, MLA_QK)
    p["wq"] = _head_slots(wuq.reshape(MLA_Q_RANK, -1), MLA_QK).astype(BF16)
    p["wqs"] = _head_slots(_swap_rope_halves(wuq).reshape(MLA_Q_RANK, -1), MLA_QK).astype(BF16)
    wukv = mla_w_ukv[lyr].reshape(MLA_KV_RANK, MLA_HEADS, MLA_NOPE + MLA_V)
    p["wk"] = _head_slots(wukv[..., :MLA_NOPE].reshape(MLA_KV_RANK, -1), MLA_NOPE).astype(BF16)
    wv = wukv[..., MLA_NOPE:]
    zv = jnp.zeros_like(wv)
    even = (np.arange(MLA_HEADS) % 2 == 0)[None, :, None]
    p["wv"] = jnp.where(even, jnp.concatenate([wv, zv], -1), jnp.concatenate([zv, wv], -1)).reshape(
        MLA_KV_RANK, MLA_HEADS * HEAD_SLOT).astype(BF16)
    vone = np.zeros((MLA_HEADS, HEAD_SLOT), np.float32)
    vone[0::2, MLA_V] = 1.0
    vone[1::2, 0] = 1.0
    p["vone"] = jnp.asarray(vone.reshape(1, -1))
    pad = lambda g: jnp.pad(g, (0, LANES - MLA_QK)).reshape(1, LANES)
    p["gq"] = pad(mla_q_head_g[lyr])
    p["gqs"] = pad(_swap_rope_halves(mla_q_head_g[lyr]))
    p["gk"] = pad(mla_k_head_g[lyr])
    p["gks"] = pad(_swap_rope_halves(mla_k_head_g[lyr]))

    p["w_gate"] = jnp.pad(gla_w_gate[lyr], ((0, LANES - GLA_GATE_RANK), (0, 0))).astype(BF16)
    p["b_gate"] = gla_b_gate[lyr].reshape(1, GLA_QW)
    p["gla_og"] = jnp.tile(gla_out_g[lyr], GLA_HEADS).reshape(1, GLA_VW)

    p["w_glu"] = s5_w_glu[lyr].astype(BF16)
    p["b_glu"] = s5_b_glu[lyr].reshape(1, S5_CH)
    p["w_br_mla"] = w_br_mla[lyr].astype(BF16)
    p["w_br_gla"] = w_br_gla[lyr].astype(BF16)
    p["w_br_s5"] = w_br_s5[lyr].astype(BF16)
    p["w_out"] = w_out[lyr].astype(BF16)
    return p


def kernel(x, norm1_g, w_in, mla_q_norm_g, mla_w_uq, mla_kv_norm_g, mla_w_ukv, mla_q_head_g, mla_k_head_g,
           gla_w_gate, gla_b_gate, gla_out_g, s5_lam_re, s5_lam_im, s5_b_re, s5_b_im, s5_c_re, s5_c_im, s5_d,
           s5_log_dt, s5_w_glu, s5_b_glu, w_br_mla, w_br_gla, w_br_s5, gate_b, w_out, norm2_g, w_ff1, w_ff2):
    b, l, d = x.shape
    n = b * l
    depth = w_in.shape[0]
    tn = min(TOKEN_TILE, n)
    tl = min(TOKEN_TILE, l)
    tg = min(256, l)
    cos128, sin128 = _rope_tables(l)
    consts = _constants(tg)
    xf = x.reshape(n, d)
    for lyr in range(depth):
        p = _layer_params(lyr, w_in, gate_b, mla_q_norm_g, mla_w_uq, mla_kv_norm_g, mla_w_ukv, mla_q_head_g,
                          mla_k_head_g, gla_w_gate, gla_b_gate, gla_out_g, s5_w_glu, s5_b_glu, w_br_mla,
                          w_br_gla, w_br_s5, w_out)
        sp = _s5_params(s5_lam_re[lyr], s5_lam_im[lyr], s5_b_re[lyr], s5_b_im[lyr], s5_c_re[lyr], s5_c_im[lyr],
                        s5_d[lyr], s5_log_dt[lyr])
        zmla, zgla, su, gates = _in_proj(xf, norm1_g[lyr].reshape(1, d), p["w_in"], p["gate_b"], tn)
        q, k, v = _mla_prep(zmla.reshape(b, l, ZMLA_W), cos128, sin128, p, tl)
        oa = _flash(q, k, v, min(FLASH_TQ, l)).reshape(n, BRANCH_W)
        ob = _gla(zgla.reshape(b, l, ZGLA_W), p, consts, tg).reshape(n, BRANCH_W)
        y = _s5(su, sp, b, l)
        x1 = _merge(xf, oa, ob, y, gates, p, tn)
        xf = _ffn(x1, norm2_g[lyr].reshape(1, d), w_ff1[lyr].astype(BF16), w_ff2[lyr].astype(BF16), tn)
    return xf.reshape(b, l, d)
```
